```python
import math
import jax, jax.numpy as jnp
from jax import lax
import numpy as np

D_MODEL = 2048
BATCH = 4
SEQ = 2048
DEPTH = 1
DEC_BATCH = 1
DEC_SEQ = 16384
PAST_LEN = 128

ATT_HEADS = 16
ATT_KV_HEADS = 4
HEAD_DIM = 128
WINDOW = 128
BLOCK = 128
ROPE_THETA = 10000.0
GLA_HEADS = 4
GLA_DK = D_MODEL // 2 // GLA_HEADS
GLA_DV = D_MODEL // GLA_HEADS
GLA_GATE_RANK = 16
GLA_TAU = 16.0
GLA_CHUNK = 64
PEER_HEADS = 8
PEER_NKEYS = 128
PEER_N = PEER_NKEYS * PEER_NKEYS
PEER_QDIM = 256
PEER_TOPK = 16
PEER_TOKEN_BLOCK = 128
DN_ALPHA = (2.0 * DEPTH) ** 0.25
DN_BETA = (8.0 * DEPTH) ** -0.25
LN_EPS = 1e-5
RMS_EPS = 1e-6

ATT_Q = ATT_HEADS * HEAD_DIM
ATT_KV = ATT_KV_HEADS * HEAD_DIM
GLA_K = GLA_HEADS * GLA_DK
GLA_V = GLA_HEADS * GLA_DV
IN_WIDTHS = (ATT_Q, ATT_KV, ATT_KV, GLA_K, GLA_K, GLA_V, GLA_GATE_RANK, GLA_GATE_RANK, GLA_V, D_MODEL, D_MODEL)
IN_WIDTH = sum(IN_WIDTHS)

kernel_name = "hybrid_bidir_swa_gla_peer_encoder"

F32 = jnp.float32


def layer_norm(x, g=None, b=None):
    xf = x.astype(F32)
    mu = jnp.mean(xf, -1, keepdims=True)
    var = jnp.mean(jnp.square(xf - mu), -1, keepdims=True)
    y = (xf - mu) * lax.rsqrt(var + LN_EPS)
    if g is not None:
        y = y * g.astype(F32) + b.astype(F32)
    return y.astype(x.dtype)


def rope(t):
    S = t.shape[1]
    half = HEAD_DIM // 2
    inv = ROPE_THETA ** (-jnp.arange(half, dtype=F32) / half)
    ang = jnp.arange(S, dtype=F32)[:, None] * inv[None, :]
    cos = jnp.cos(ang)[None, :, None, :]
    sin = jnp.sin(ang)[None, :, None, :]
    t1 = t[..., :half].astype(F32)
    t2 = t[..., half:].astype(F32)
    return jnp.concatenate([t1 * cos - t2 * sin, t2 * cos + t1 * sin], -1).astype(t.dtype)


def window_attention(q, k, v, sink):
    B, S = q.shape[:2]
    nb = S // BLOCK
    G = ATT_HEADS // ATT_KV_HEADS
    qb = q.reshape(B, nb, BLOCK, ATT_KV_HEADS, G, HEAD_DIM)

    def band(t):
        tb = t.reshape(B, nb, BLOCK, ATT_KV_HEADS, HEAD_DIM)
        tp = jnp.pad(tb, ((0, 0), (1, 1), (0, 0), (0, 0), (0, 0)))
        return jnp.concatenate([tp[:, :-2], tp[:, 1:-1], tp[:, 2:]], axis=2)

    kb, vb = band(k), band(v)
    s = jnp.einsum('bnqhgd,bnkhd->bnhgqk', qb, kb, preferred_element_type=F32) * (HEAD_DIM ** -0.5)
    blk = jnp.arange(nb)[:, None, None]
    qpos = blk * BLOCK + jnp.arange(BLOCK)[None, :, None]
    kpos = (blk - 1) * BLOCK + jnp.arange(3 * BLOCK)[None, None, :]
    valid = (jnp.abs(qpos - kpos) <= WINDOW) & (kpos >= 0) & (kpos < S)
    s = jnp.where(valid[None, :, None, None], s, -jnp.inf)
    sink_l = sink.astype(F32).reshape(ATT_KV_HEADS, G)[None, None, :, :, None, None]
    m = jnp.maximum(jnp.max(s, -1, keepdims=True), sink_l)
    p = jnp.exp(s - m)
    p = p / (jnp.sum(p, -1, keepdims=True) + jnp.exp(sink_l - m))
    o = jnp.einsum('bnhgqk,bnkhd->bnqhgd', p.astype(v.dtype), vb)
    return o.reshape(B, S, ATT_Q)


def gla_scan(q, k, v, logg):
    B, S, H, dk = q.shape
    dv = v.shape[-1]
    nc = S // GLA_CHUNK

    def to_chunks(t):
        return t.reshape(B, nc, GLA_CHUNK, H, t.shape[-1]).transpose(1, 0, 3, 2, 4).astype(F32)

    qc, kc, vc, gc = to_chunks(q), to_chunks(k), to_chunks(v), to_chunks(logg)
    causal = jnp.tril(jnp.ones((GLA_CHUNK, GLA_CHUNK), bool))[:, :, None]

    def step(state, inp):
        qi, ki, vi, gi = inp
        b = jnp.cumsum(gi, axis=2)
        o_inter = jnp.einsum('bhld,bhdv->bhlv', qi * jnp.exp(b), state)
        diff = b[:, :, :, None, :] - b[:, :, None, :, :]
        decay = jnp.exp(jnp.where(causal, diff, -jnp.inf))
        att = jnp.einsum('bhijd,bhjd->bhij', decay * qi[:, :, :, None, :], ki)
        o = o_inter + jnp.einsum('bhij,bhjv->bhiv', att, vi)
        bL = b[:, :, -1:, :]
        state = jnp.exp(bL[:, :, 0, :])[..., None] * state + jnp.einsum('bhld,bhlv->bhdv', ki * jnp.exp(bL - b), vi)
        return state, o

    state0 = jnp.zeros((B, H, dk, dv), F32)
    _, oc = lax.scan(step, state0, (qc, kc, vc, gc))
    return oc.transpose(1, 0, 3, 2, 4).reshape(B, S, H, dv)


def gla_gate(low, w2, b2):
    z = jnp.einsum('bsr,rk->bsk', low, w2) + b2
    B, S = low.shape[:2]
    return (jax.nn.log_sigmoid(z.astype(F32)) / GLA_TAU).reshape(B, S, GLA_HEADS, GLA_DK)


def peer(h, w_query, subkeys, u_table, v_table):
    B, S, D = h.shape
    T = B * S
    hf = h.reshape(T, D)
    q = (hf @ w_query).reshape(T, PEER_HEADS, 2, PEER_QDIM // 2)
    sc = jnp.einsum('thpd,hpnd->thpn', q, subkeys).astype(F32)
    sv, si = lax.top_k(sc, PEER_TOPK)
    cand = sv[:, :, 0, :, None] + sv[:, :, 1, None, :]
    cidx = si[:, :, 0, :, None] * PEER_NKEYS + si[:, :, 1, None, :]
    fv, fi = lax.top_k(cand.reshape(T, PEER_HEADS, PEER_TOPK * PEER_TOPK), PEER_TOPK)
    eidx = jnp.take_along_axis(cidx.reshape(T, PEER_HEADS, PEER_TOPK * PEER_TOPK), fi, axis=-1)
    gate = jax.nn.softmax(fv, axis=-1)
    nblk = T // PEER_TOKEN_BLOCK

    def block_fn(args):
        hb, eb, gb = args
        ub = jnp.take(u_table, eb, axis=0)
        a = jnp.einsum('td,thkd->thk', hb, ub)
        w = (gb * jax.nn.gelu(a.astype(F32), approximate=False)).astype(hb.dtype)
        vb = jnp.take(v_table, eb, axis=0)
        return jnp.einsum('thk,thkd->td', w, vb)

    out = lax.map(block_fn, (hf.reshape(nblk, PEER_TOKEN_BLOCK, D),
                             eidx.reshape(nblk, PEER_TOKEN_BLOCK, PEER_HEADS, PEER_TOPK),
                             gate.reshape(nblk, PEER_TOKEN_BLOCK, PEER_HEADS, PEER_TOPK)))
    return out.reshape(B, S, D)


def encoder_layer(x, c, w_ada, b_ada, w_in, attn_sink, gla_w2_fwd, gla_b_fwd, gla_w2_bwd, gla_b_bwd,
                  gla_norm_w, w_branch_attn, w_branch_gla, w_out, ln1_g, ln1_b,
                  peer_w_query, peer_subkeys, peer_u, peer_v, ln2_g, ln2_b):
    B, S, D = x.shape
    mod = (jax.nn.silu(c) @ w_ada + b_ada)[:, None, :]
    sh1, sc1, g1, sh2, sc2, g2 = jnp.split(mod, 6, axis=-1)

    h = layer_norm(x) * (1 + sc1) + sh1
    z = h @ w_in
    split_pts = [int(p) for p in np.cumsum(IN_WIDTHS)[:-1]]
    aq, ak, av, gq, gk, gv, glow_f, glow_b, gr, ga, gb = jnp.split(z, split_pts, axis=-1)

    aq = rope(aq.reshape(B, S, ATT_HEADS, HEAD_DIM))
    ak = rope(ak.reshape(B, S, ATT_KV_HEADS, HEAD_DIM))
    av = av.reshape(B, S, ATT_KV_HEADS, HEAD_DIM)
    o_attn = window_attention(aq, ak, av, attn_sink)

    gq = gq.reshape(B, S, GLA_HEADS, GLA_DK) * (GLA_DK ** -0.5)
    gk = gk.reshape(B, S, GLA_HEADS, GLA_DK)
    gv = gv.reshape(B, S, GLA_HEADS, GLA_DV)
    logg_f = gla_gate(glow_f, gla_w2_fwd, gla_b_fwd)
    logg_b = gla_gate(glow_b, gla_w2_bwd, gla_b_bwd)
    o_f = gla_scan(gq, gk, gv, logg_f)
    flip = lambda t: jnp.flip(t, axis=1)
    o_b = flip(gla_scan(flip(gq), flip(gk), flip(gv), flip(logg_b)))
    o_g = o_f + o_b
    o_g = o_g * lax.rsqrt(jnp.mean(jnp.square(o_g), -1, keepdims=True) + RMS_EPS)
    o_g = o_g * gla_norm_w.astype(F32).reshape(GLA_HEADS, GLA_DV)
    o_g = (o_g.reshape(B, S, GLA_V) * jax.nn.silu(gr.astype(F32))).astype(x.dtype)

    merged = jax.nn.sigmoid(ga) * (o_attn @ w_branch_attn) + jax.nn.sigmoid(gb) * (o_g @ w_branch_gla)
    mix = merged @ w_out
    x = layer_norm(DN_ALPHA * x + g1 * mix, ln1_g, ln1_b)

    h2 = layer_norm(x) * (1 + sc2) + sh2
    ff = peer(h2, peer_w_query, peer_subkeys, peer_u, peer_v)
    x = layer_norm(DN_ALPHA * x + g2 * ff, ln2_g, ln2_b)
    return x


def setup_inputs(seed: int = 0) -> dict:
    key = jax.random.key(seed)
    ks = jax.random.split(key, 24)
    D = D_MODEL
    n = lambda k, shape, s: jax.random.normal(k, shape, F32) * s
    return {
        "x_prompt": n(ks[0], (BATCH, SEQ, D), 1.0),
        "x_sample": n(ks[1], (DEC_BATCH, DEC_SEQ, D), 1.0),
        "c_prompt": n(ks[2], (BATCH, D), 1.0),
        "c_sample": n(ks[3], (DEC_BATCH, D), 1.0),
        "w_ada": n(ks[4], (DEPTH, D, 6 * D), D ** -0.5),
        "b_ada": n(ks[5], (DEPTH, 6 * D), 0.01),
        "w_in": n(ks[6], (DEPTH, D, IN_WIDTH), D ** -0.5),
        "attn_sink": n(ks[7], (DEPTH, ATT_HEADS), 1.0),
        "gla_w2_fwd": n(ks[8], (DEPTH, GLA_GATE_RANK, GLA_K), GLA_GATE_RANK ** -0.5),
        "gla_b_fwd": n(ks[9], (DEPTH, GLA_K), 0.1),
        "gla_w2_bwd": n(ks[10], (DEPTH, GLA_GATE_RANK, GLA_K), GLA_GATE_RANK ** -0.5),
        "gla_b_bwd": n(ks[11], (DEPTH, GLA_K), 0.1),
        "gla_norm_w": 1.0 + n(ks[12], (DEPTH, GLA_V), 0.02),
        "w_branch_attn": n(ks[13], (DEPTH, ATT_Q, D), DN_BETA * ATT_Q ** -0.5),
        "w_branch_gla": n(ks[14], (DEPTH, GLA_V, D), DN_BETA * GLA_V ** -0.5),
        "w_out": n(ks[15], (DEPTH, D, D), DN_BETA * D ** -0.5),
        "ln1_g": 1.0 + n(ks[16], (DEPTH, D), 0.02),
        "ln1_b": n(ks[17], (DEPTH, D), 0.02),
        "peer_w_query": n(ks[18], (DEPTH, D, PEER_HEADS * PEER_QDIM), D ** -0.5),
        "peer_subkeys": n(ks[19], (DEPTH, PEER_HEADS, 2, PEER_NKEYS, PEER_QDIM // 2), (PEER_QDIM // 2) ** -0.5),
        "peer_u": n(ks[20], (DEPTH, PEER_N, D), D ** -0.5),
        "peer_v": n(ks[21], (DEPTH, PEER_N, D), DN_BETA),
        "ln2_g": 1.0 + n(ks[22], (DEPTH, D), 0.02),
        "ln2_b": n(ks[23], (DEPTH, D), 0.02),
    }


def reference(x_prompt, x_sample, c_prompt, c_sample, w_ada, b_ada, w_in, attn_sink,
              gla_w2_fwd, gla_b_fwd, gla_w2_bwd, gla_b_bwd, gla_norm_w, w_branch_attn, w_branch_gla,
              w_out, ln1_g, ln1_b, peer_w_query, peer_subkeys, peer_u, peer_v, ln2_g, ln2_b):
    def run(x, c):
        for l in range(DEPTH):
            x = encoder_layer(x, c, w_ada[l], b_ada[l], w_in[l], attn_sink[l],
                              gla_w2_fwd[l], gla_b_fwd[l], gla_w2_bwd[l], gla_b_bwd[l], gla_norm_w[l],
                              w_branch_attn[l], w_branch_gla[l], w_out[l], ln1_g[l], ln1_b[l],
                              peer_w_query[l], peer_subkeys[l], peer_u[l], peer_v[l], ln2_g[l], ln2_b[l])
        return x

    y_prompt = run(x_prompt, c_prompt)
    y_sample = run(x_sample, c_sample)
    return (y_prompt, y_sample)
```

```python
import functools
import math

import numpy as np
import jax
import jax.numpy as jnp
from jax import lax
from jax.experimental import pallas as pl
from jax.experimental.pallas import tpu as pltpu

F32 = jnp.float32
BF16 = jnp.bfloat16

D_MODEL = 2048
ATT_HEADS = 16
ATT_KV_HEADS = 4
ATT_GROUP = ATT_HEADS // ATT_KV_HEADS
HEAD_DIM = 128
WINDOW_BLOCK = 128
ROPE_THETA = 10000.0
GLA_HEADS = 4
GLA_DK = 256
GLA_DV = 512
GLA_GATE_RANK = 16
GLA_TAU = 16.0
PEER_HEADS = 8
PEER_NKEYS = 128
PEER_N = PEER_NKEYS * PEER_NKEYS
PEER_TOPK = 16
DN_ALPHA = 2.0 ** 0.25
LN_EPS = 1e-5
RMS_EPS = 1e-6

ATT_Q = ATT_HEADS * HEAD_DIM
ATT_KV = ATT_KV_HEADS * HEAD_DIM
GLA_K = GLA_HEADS * GLA_DK
GLA_V = GLA_HEADS * GLA_DV
IN_WIDTHS = (ATT_Q, ATT_KV, ATT_KV, GLA_K, GLA_K, GLA_V, GLA_GATE_RANK, GLA_GATE_RANK, GLA_V, D_MODEL, D_MODEL)

Z_AQ = 0
Z_GV = 2048
Z_GR = 4096
Z_GA = 6144
Z_GB = 8192
Z_GQ = 10240
Z_GK = 11264
Z_AK = 12288
Z_AV = 12800
Z_GLOW = 13312
Z_WIDTH = 13440

LANES = 128
VMEM_LIMIT = 56 * 1024 * 1024

NEG_BIG = -1e30


def _ln(x):
    mu = jnp.mean(x, axis=-1, keepdims=True)
    xc = x - mu
    var = jnp.mean(xc * xc, axis=-1, keepdims=True)
    return xc * lax.rsqrt(var + LN_EPS)


def _dot_nt(a, b):
    return lax.dot_general(a, b, (((1,), (1,)), ((), ())), preferred_element_type=F32)


def _dot_tn(a, b):
    return lax.dot_general(a, b, (((0,), (0,)), ((), ())), preferred_element_type=F32)


def _dot(a, b):
    return jnp.dot(a, b, preferred_element_type=F32)


def _ada_kernel(c_ref, w_ref, b_ref, o_ref):
    c = c_ref[...]
    a = (c * jax.nn.sigmoid(c)).astype(BF16)
    o_ref[...] = _dot(a, w_ref[...].astype(BF16)) + b_ref[...]


def _ada(c_pad, w_ada, b_ada, tn=1024):
    rows, d = c_pad.shape
    n = w_ada.shape[1]
    return pl.pallas_call(
        _ada_kernel,
        grid=(n // tn,),
        in_specs=[
            pl.BlockSpec((rows, d), lambda j: (0, 0)),
            pl.BlockSpec((d, tn), lambda j: (0, j)),
            pl.BlockSpec((1, tn), lambda j: (0, j)),
        ],
        out_specs=pl.BlockSpec((rows, tn), lambda j: (0, j)),
        out_shape=jax.ShapeDtypeStruct((rows, n), F32),
        compiler_params=pltpu.CompilerParams(dimension_semantics=("arbitrary",), vmem_limit_bytes=VMEM_LIMIT),
        name="ada_mod",
    )(c_pad, w_ada, b_ada.reshape(1, n))


def _lnwin_kernel(seq_ref, x_ref, sh_ref, sc_ref, w_ref, o_ref, h_ref):
    @pl.when(pl.program_id(1) == 0)
    def _():
        y = _ln(x_ref[...])
        h_ref[...] = (y * (1.0 + sc_ref[0]) + sh_ref[0]).astype(BF16)

    o_ref[...] = _dot(h_ref[...], w_ref[...]).astype(o_ref.dtype)


def _ln_win(x, mod3, w_in_r, tile_seq, tm, tn):
    t, d = x.shape
    n = w_in_r.shape[1]
    grid_spec = pltpu.PrefetchScalarGridSpec(
        num_scalar_prefetch=1,
        grid=(t // tm, n // tn),
        in_specs=[
            pl.BlockSpec((tm, d), lambda i, j, s: (i, 0)),
            pl.BlockSpec((1, 1, d), lambda i, j, s: (s[i] * 6 + 0, 0, 0)),
            pl.BlockSpec((1, 1, d), lambda i, j, s: (s[i] * 6 + 1, 0, 0)),
            pl.BlockSpec((d, tn), lambda i, j, s: (0, j)),
        ],
        out_specs=pl.BlockSpec((tm, tn), lambda i, j, s: (i, j)),
        scratch_shapes=[pltpu.VMEM((tm, d), BF16)],
    )
    return pl.pallas_call(
        _lnwin_kernel,
        grid_spec=grid_spec,
        out_shape=jax.ShapeDtypeStruct((t, n), BF16),
        compiler_params=pltpu.CompilerParams(
            dimension_semantics=("arbitrary", "arbitrary"), vmem_limit_bytes=VMEM_LIMIT),
        name="ln_win",
    )(tile_seq, x, mod3, mod3, w_in_r)


def _rope(t, cosf, sinf):
    return t * cosf + pltpu.roll(t, HEAD_DIM // 2, 1) * sinf


def _attn_kernel(first_ref, last_ref, pblk_ref, sink_ref, q_ref, kp_ref, kc_ref, kn_ref,
                 vp_ref, vc_ref, vn_ref, cp_ref, cc_ref, cn_ref, sp_ref, sc_ref, sn_ref, o_ref):
    n = pl.program_id(0)
    blk = WINDOW_BLOCK
    is_first = first_ref[n] == 1
    is_last = last_ref[n] == 1
    col_lo = jnp.where(is_first, blk, 0)
    col_hi = jnp.where(is_last, 2 * blk, 3 * blk)
    rows = ATT_GROUP * blk
    r = lax.broadcasted_iota(jnp.int32, (rows, 3 * blk), 0) & (blk - 1)
    c = lax.broadcasted_iota(jnp.int32, (rows, 3 * blk), 1)
    d = c - r
    valid = (d >= 0) & (d <= 2 * blk) & (c >= col_lo) & (c < col_hi)
    grp = lax.broadcasted_iota(jnp.int32, (rows, 1), 0) // blk
    cosc, sinc = cc_ref[...], sc_ref[...]
    scale = HEAD_DIM ** -0.5
    for g in range(ATT_KV_HEADS):
        ks = slice(g * HEAD_DIM, (g + 1) * HEAD_DIM)
        kband = jnp.concatenate([
            _rope(kp_ref[:, ks].astype(F32), cp_ref[...], sp_ref[...]).astype(BF16),
            _rope(kc_ref[:, ks].astype(F32), cosc, sinc).astype(BF16),
            _rope(kn_ref[:, ks].astype(F32), cn_ref[...], sn_ref[...]).astype(BF16)], axis=0)
        vband = jnp.concatenate([vp_ref[:, ks], vc_ref[:, ks], vn_ref[:, ks]], axis=0)
        qs = []
        sink_col = jnp.zeros((rows, 1), F32)
        for j in range(ATT_GROUP):
            h = g * ATT_GROUP + j
            qh = q_ref[:, h * HEAD_DIM:(h + 1) * HEAD_DIM].astype(F32)
            qs.append((_rope(qh, cosc, sinc) * scale).astype(BF16))
            sink_col = jnp.where(grp == j, sink_ref[h], sink_col)
        qg = jnp.concatenate(qs, axis=0)
        s = _dot_nt(qg, kband)
        s = jnp.where(valid, s, NEG_BIG)
        m = jnp.maximum(jnp.max(s, axis=-1, keepdims=True), sink_col)
        p = jnp.exp(s - m)
        den = jnp.sum(p, axis=-1, keepdims=True) + jnp.exp(sink_col - m)
        o = _dot(p.astype(BF16), vband) / den
        for j in range(ATT_GROUP):
            h = g * ATT_GROUP + j
            o_ref[:, h * HEAD_DIM:(h + 1) * HEAD_DIM] = o[j * blk:(j + 1) * blk].astype(o_ref.dtype)


def _attention(z, sink, cosf, sinf, blk_first, blk_last, blk_pos):
    t = z.shape[0]
    blk = WINDOW_BLOCK
    nblk = t // blk
    npos = cosf.shape[0] // blk
    kcol = Z_AK // ATT_KV
    vcol = Z_AV // ATT_KV
    prev = lambda n: jnp.maximum(n - 1, 0)
    nxt = lambda n: jnp.minimum(n + 1, nblk - 1)
    pprev = lambda p: jnp.maximum(p - 1, 0)
    pnxt = lambda p: jnp.minimum(p + 1, npos - 1)
    tab = lambda f: pl.BlockSpec((blk, HEAD_DIM), lambda n, a, b, p: (f(p[n]), 0))
    same = lambda p: p
    grid_spec = pltpu.PrefetchScalarGridSpec(
        num_scalar_prefetch=3,
        grid=(nblk,),
        in_specs=[
            pl.BlockSpec(memory_space=pltpu.SMEM),
            pl.BlockSpec((blk, ATT_Q), lambda n, a, b, p: (n, Z_AQ // ATT_Q)),
            pl.BlockSpec((blk, ATT_KV), lambda n, a, b, p: (prev(n), kcol)),
            pl.BlockSpec((blk, ATT_KV), lambda n, a, b, p: (n, kcol)),
            pl.BlockSpec((blk, ATT_KV), lambda n, a, b, p: (nxt(n), kcol)),
            pl.BlockSpec((blk, ATT_KV), lambda n, a, b, p: (prev(n), vcol)),
            pl.BlockSpec((blk, ATT_KV), lambda n, a, b, p: (n, vcol)),
            pl.BlockSpec((blk, ATT_KV), lambda n, a, b, p: (nxt(n), vcol)),
            tab(pprev), tab(same), tab(pnxt),
            tab(pprev), tab(same), tab(pnxt),
        ],
        out_specs=pl.BlockSpec((blk, ATT_Q), lambda n, a, b, p: (n, 0)),
    )
    return pl.pallas_call(
        _attn_kernel,
        grid_spec=grid_spec,
        out_shape=jax.ShapeDtypeStruct((t, ATT_Q), BF16),
        compiler_params=pltpu.CompilerParams(dimension_semantics=("arbitrary",), vmem_limit_bytes=VMEM_LIMIT),
        name="win_attn",
    )(blk_first, blk_last, blk_pos, sink, z, z, z, z, z, z, z, cosf, cosf, cosf, sinf, sinf, sinf)


GLA_SUB = 16


def _gla_chunk(q, k, v, glow, w2, bias, st_ref, hd, reset, reverse):
    c = q.shape[0]
    sub = GLA_SUB
    z = _dot(glow, w2) + bias
    g = -(jnp.maximum(-z, 0.0) + jnp.log1p(jnp.exp(-jnp.abs(z)))) * (1.0 / GLA_TAU)
    ri = lax.broadcasted_iota(jnp.int32, (c, c), 0)
    ci = lax.broadcasted_iota(jnp.int32, (c, c), 1)
    tri = jnp.where((ci >= ri) if reverse else (ci <= ri), 1.0, 0.0).astype(BF16)
    g_hi = g.astype(BF16)
    g_lo = (g - g_hi.astype(F32)).astype(BF16)
    b = _dot(tri, g_hi) + _dot(tri, g_lo)
    qf = q.astype(F32) * (GLA_DK ** -0.5)
    kf = k.astype(F32)
    state = jnp.where(reset, 0.0, st_ref[hd])
    o_inter = _dot_nt((qf * jnp.exp(b)).astype(BF16), state.astype(BF16))
    parts = []
    for i in range(c // sub):
        r0, r1 = i * sub, (i + 1) * sub
        if reverse:
            lo, hi = r0, c
            ref = b[r1 - 1:r1]
        else:
            lo, hi = 0, r1
            ref = b[r0:r0 + 1]
        qi = (qf[r0:r1] * jnp.exp(b[r0:r1] - ref)).astype(BF16)
        ki = (kf[lo:hi] * jnp.exp(ref - b[lo:hi])).astype(BF16)
        a = _dot_nt(qi, ki)
        rr = lax.broadcasted_iota(jnp.int32, a.shape, 0) + r0
        cc = lax.broadcasted_iota(jnp.int32, a.shape, 1) + lo
        a = jnp.where((cc >= rr) if reverse else (cc <= rr), a, 0.0)
        parts.append(_dot(a.astype(BF16), v[lo:hi]))
    o = o_inter + jnp.concatenate(parts, axis=0)
    edge = b[0:1] if reverse else b[c - 1:c]
    kd = (kf * jnp.exp(edge - b)).astype(BF16)
    st_ref[hd] = jnp.exp(edge) * state + _dot_tn(v, kd)
    return o


def _gla_kernel(first_ref, last_ref, qf_ref, kf_ref, vf_ref, lf_ref, qb_ref, kb_ref, vb_ref, lb_ref,
                w2f_ref, bf_ref, w2b_ref, bb_ref, of_ref, ob_ref, sf_ref, sb_ref, *, heads):
    cidx = pl.program_id(1)
    nch = pl.num_programs(1)
    reset_f = first_ref[cidx] == 1
    reset_b = last_ref[nch - 1 - cidx] == 1
    glow_f = lf_ref[:, 0:GLA_GATE_RANK]
    glow_b = lb_ref[:, GLA_GATE_RANK:2 * GLA_GATE_RANK]
    for hd in range(heads):
        ksl = slice(hd * GLA_DK, (hd + 1) * GLA_DK)
        vsl = slice(hd * GLA_DV, (hd + 1) * GLA_DV)
        of_ref[:, vsl] = _gla_chunk(qf_ref[:, ksl], kf_ref[:, ksl], vf_ref[:, vsl], glow_f,
                                    w2f_ref[:, ksl], bf_ref[:, ksl], sf_ref, hd, reset_f, False)
        ob_ref[:, vsl] = _gla_chunk(qb_ref[:, ksl], kb_ref[:, ksl], vb_ref[:, vsl], glow_b,
                                    w2b_ref[:, ksl], bb_ref[:, ksl], sb_ref, hd, reset_b, True)


def _gla(z, w2f, bf, w2b, bb, ch_first, ch_last, chunk, heads_per_step):
    t = z.shape[0]
    nch = t // chunk
    hs = heads_per_step
    ngrp = GLA_HEADS // hs
    kw, vw = hs * GLA_DK, hs * GLA_DV
    fwd = lambda c: c
    bwd = lambda c: nch - 1 - c
    zq = lambda f: pl.BlockSpec((chunk, kw), lambda h, c, a, b: (f(c), Z_GQ // kw + h))
    zk = lambda f: pl.BlockSpec((chunk, kw), lambda h, c, a, b: (f(c), Z_GK // kw + h))
    zv = lambda f: pl.BlockSpec((chunk, vw), lambda h, c, a, b: (f(c), Z_GV // vw + h))
    zl = lambda f: pl.BlockSpec((chunk, LANES), lambda h, c, a, b: (f(c), Z_GLOW // LANES))
    wspec = pl.BlockSpec((GLA_GATE_RANK, kw), lambda h, c, a, b: (0, h))
    bspec = pl.BlockSpec((1, kw), lambda h, c, a, b: (0, h))
    grid_spec = pltpu.PrefetchScalarGridSpec(
        num_scalar_prefetch=2,
        grid=(ngrp, nch),
        in_specs=[zq(fwd), zk(fwd), zv(fwd), zl(fwd), zq(bwd), zk(bwd), zv(bwd), zl(bwd),
                  wspec, bspec, wspec, bspec],
        out_specs=[pl.BlockSpec((chunk, vw), lambda h, c, a, b: (c, h)),
                   pl.BlockSpec((chunk, vw), lambda h, c, a, b: (nch - 1 - c, h))],
        scratch_shapes=[pltpu.VMEM((hs, GLA_DV, GLA_DK), F32), pltpu.VMEM((hs, GLA_DV, GLA_DK), F32)],
    )
    return pl.pallas_call(
        functools.partial(_gla_kernel, heads=hs),
        grid_spec=grid_spec,
        out_shape=[jax.ShapeDtypeStruct((t, GLA_V), F32), jax.ShapeDtypeStruct((t, GLA_V), F32)],
        compiler_params=pltpu.CompilerParams(
            dimension_semantics=("arbitrary", "arbitrary"), vmem_limit_bytes=VMEM_LIMIT),
        name="gla_bidir",
    )(ch_first, ch_last, z, z, z, z, z, z, z, z, w2f, bf, w2b, bb)


def _merge_kernel(seq_ref, oa_ref, of_ref, ob_ref, gr_ref, ga_ref, gb_ref, x_ref, nw_ref, wa_ref, wb_ref,
                  wo_ref, g1_ref, l1g_ref, l1b_ref, sh2_ref, sc2_ref, x1_ref, h2_ref):
    og = of_ref[...] + ob_ref[...]
    parts = []
    for h in range(GLA_HEADS):
        th = og[:, h * GLA_DV:(h + 1) * GLA_DV]
        ms = jnp.mean(th * th, axis=-1, keepdims=True)
        parts.append(th * lax.rsqrt(ms + RMS_EPS))
    gr = gr_ref[...].astype(F32)
    ogn = jnp.concatenate(parts, axis=1) * nw_ref[...] * (gr * jax.nn.sigmoid(gr))
    a = _dot(oa_ref[...], wa_ref[...])
    b = _dot(ogn.astype(BF16), wb_ref[...])
    merged = jax.nn.sigmoid(ga_ref[...].astype(F32)) * a + jax.nn.sigmoid(gb_ref[...].astype(F32)) * b
    mix = _dot(merged.astype(BF16), wo_ref[...])
    x1 = _ln(DN_ALPHA * x_ref[...] + g1_ref[0] * mix) * l1g_ref[...] + l1b_ref[...]
    x1_ref[...] = x1
    h2_ref[...] = (_ln(x1) * (1.0 + sc2_ref[0]) + sh2_ref[0]).astype(BF16)


def _merge(o_attn, o_f, o_b, z, x, mod3, norm_w, wa, wb, wo, ln1_g, ln1_b, tile_seq, tm):
    t, d = x.shape
    tok = lambda col: pl.BlockSpec((tm, d), lambda i, s: (i, col))
    const = lambda shape: pl.BlockSpec(shape, lambda i, s: (0,) * len(shape), pipeline_mode=pl.Buffered(1))
    modspec = lambda k: pl.BlockSpec((1, 1, d), lambda i, s: (s[i] * 6 + k, 0, 0))
    grid_spec = pltpu.PrefetchScalarGridSpec(
        num_scalar_prefetch=1,
        grid=(t // tm,),
        in_specs=[tok(0), tok(0), tok(0), tok(Z_GR // d), tok(Z_GA // d), tok(Z_GB // d), tok(0),
                  const((1, d)), const((d, d)), const((d, d)), const((d, d)),
                  modspec(2), const((1, d)), const((1, d)), modspec(3), modspec(4)],
        out_specs=[tok(0), tok(0)],
    )
    return pl.pallas_call(
        _merge_kernel,
        grid_spec=grid_spec,
        out_shape=[jax.ShapeDtypeStruct((t, d), F32), jax.ShapeDtypeStruct((t, d), BF16)],
        compiler_params=pltpu.CompilerParams(dimension_semantics=("arbitrary",), vmem_limit_bytes=VMEM_LIMIT),
        name="merge_ln1",
    )(tile_seq, o_attn, o_f, o_b, z, z, z, x, norm_w, wa, wb, wo, mod3, ln1_g, ln1_b, mod3, mod3)


PEER_EXTRACT = PEER_TOPK + 1
SLAB = 8


def _slabs(s):
    return [s[i * SLAB:(i + 1) * SLAB] for i in range(s.shape[0] // SLAB)]


def _extract_desc(slabs, count):
    vals = []
    for _ in range(count):
        m8 = functools.reduce(jnp.maximum, slabs)
        m = jnp.max(m8, axis=0, keepdims=True)
        vals.append(m)
        slabs = [jnp.where(sl == m, -jnp.inf, sl) for sl in slabs]
    return vals


def _stack_rows(vals, rows, lanes):
    ridx = lax.broadcasted_iota(jnp.int32, (rows, lanes), 0)
    acc = jnp.full((rows, lanes), -jnp.inf, F32)
    for i, v in enumerate(vals):
        acc = jnp.where(ridx == i, v, acc)
    return acc


def _router_kernel(h2_ref, wq_ref, sk_ref, s2_ref, e2_ref, thr_ref, c_ref):
    tm = h2_ref.shape[0]
    kx = PEER_EXTRACT
    q = _dot(h2_ref[...], wq_ref[...]).astype(BF16)
    ridx = lax.broadcasted_iota(jnp.int32, (SLAB, tm), 0)
    for h in range(PEER_HEADS):
        s1 = _dot_nt(sk_ref[h, 0], q[:, (2 * h) * PEER_NKEYS:(2 * h + 1) * PEER_NKEYS])
        s2 = _dot_nt(sk_ref[h, 1], q[:, (2 * h + 1) * PEER_NKEYS:(2 * h + 2) * PEER_NKEYS])
        v1 = _extract_desc(_slabs(s1), kx)
        v2 = _extract_desc(_slabs(s2), kx)
        st1 = _stack_rows(v1, 24, tm)
        st2 = _stack_rows(v2, 24, tm)
        cand = []
        for a in range(SLAB):
            nb = kx // (a + 1)
            for sl in range((nb + SLAB - 1) // SLAB):
                piece = st2[sl * SLAB:(sl + 1) * SLAB] + v1[a]
                if nb < (sl + 1) * SLAB:
                    piece = jnp.where(ridx < nb - sl * SLAB, piece, -jnp.inf)
                cand.append(piece)
        cand.append(st1[SLAB:2 * SLAB] + v2[0])
        cand.append(jnp.where(ridx < kx - 2 * SLAB, st1[2 * SLAB:3 * SLAB] + v2[0], -jnp.inf))
        top = _extract_desc(cand, kx)
        zsum = functools.reduce(lambda acc, v: acc + jnp.exp(v - top[0]), top[:PEER_TOPK], jnp.zeros_like(top[0]))
        tau = 0.5 * (top[PEER_TOPK - 1] + top[PEER_TOPK])
        s2_ref[h] = s2
        e2_ref[h] = jnp.exp(s2 - v2[0])
        thr_ref[h] = tau - s1
        c_ref[h] = jnp.exp(s1 - v1[0]) / zsum


def _router(h2, wq, subkeys, tm):
    t, d = h2.shape
    nq = wq.shape[1]
    out = jax.ShapeDtypeStruct((PEER_HEADS, PEER_NKEYS, t), F32)
    ospec = pl.BlockSpec((PEER_HEADS, PEER_NKEYS, tm), lambda i: (0, 0, i))
    return pl.pallas_call(
        _router_kernel,
        grid=(t // tm,),
        in_specs=[
            pl.BlockSpec((tm, d), lambda i: (i, 0)),
            pl.BlockSpec((d, nq), lambda i: (0, 0), pipeline_mode=pl.Buffered(1)),
            pl.BlockSpec(subkeys.shape, lambda i: (0, 0, 0, 0), pipeline_mode=pl.Buffered(1)),
        ],
        out_specs=[ospec, ospec, ospec, ospec],
        out_shape=[out, out, out, out],
        compiler_params=pltpu.CompilerParams(dimension_semantics=("arbitrary",), vmem_limit_bytes=VMEM_LIMIT),
        name="peer_router",
    )(h2, wq, subkeys)


_SQRT_HALF = math.sqrt(0.5)


def _gelu(x):
    return 0.5 * x * (1.0 + lax.erf(x * _SQRT_HALF))


def _peer_kernel(seq_ref, h2_ref, u_ref, vt_ref, s2_ref, e2_ref, thr_ref, c_ref, x1_ref, g2_ref, l2g_ref,
                 l2b_ref, o_ref, at_ref, wt_ref, acc_ref):
    j = pl.program_id(1)
    ni = u_ref.shape[0] // PEER_NKEYS

    @pl.when(j == 0)
    def _():
        acc_ref[...] = jnp.zeros_like(acc_ref)

    at_ref[...] = _dot_nt(u_ref[...], h2_ref[...])
    for ii in range(ni):
        rows = slice(ii * PEER_NKEYS, (ii + 1) * PEER_NKEYS)
        gate = None
        for h in range(PEER_HEADS):
            sel = jnp.where(s2_ref[h] >= thr_ref[h, ii:ii + 1, :], e2_ref[h] * c_ref[h, ii:ii + 1, :], 0.0)
            gate = sel if gate is None else gate + sel
        wt_ref[rows, :] = (gate * _gelu(at_ref[rows, :])).astype(BF16)
    acc_ref[...] += _dot(vt_ref[...], wt_ref[...])

    @pl.when(j == pl.num_programs(1) - 1)
    def _():
        ff = acc_ref[...].T
        y = DN_ALPHA * x1_ref[...] + g2_ref[0] * ff
        o_ref[...] = _ln(y) * l2g_ref[...] + l2b_ref[...]


def _peer(h2, u_bf, vt_bf, s2t, e2t, thrt, ct, x1, mod3, ln2_g, ln2_b, tile_seq, tm, tn):
    t, d = x1.shape
    ne = u_bf.shape[0]
    ni = tn // PEER_NKEYS
    const = lambda shape: pl.BlockSpec(shape, lambda i, j, s: (0,) * len(shape))
    grid_spec = pltpu.PrefetchScalarGridSpec(
        num_scalar_prefetch=1,
        grid=(t // tm, ne // tn),
        in_specs=[
            pl.BlockSpec((tm, d), lambda i, j, s: (i, 0)),
            pl.BlockSpec((tn, d), lambda i, j, s: (j, 0)),
            pl.BlockSpec((d, tn), lambda i, j, s: (0, j)),
            pl.BlockSpec((PEER_HEADS, PEER_NKEYS, tm), lambda i, j, s: (0, 0, i), pipeline_mode=pl.Buffered(1)),
            pl.BlockSpec((PEER_HEADS, PEER_NKEYS, tm), lambda i, j, s: (0, 0, i), pipeline_mode=pl.Buffered(1)),
            pl.BlockSpec((PEER_HEADS, ni, tm), lambda i, j, s: (0, j, i)),
            pl.BlockSpec((PEER_HEADS, ni, tm), lambda i, j, s: (0, j, i)),
            pl.BlockSpec((tm, d), lambda i, j, s: (i, 0), pipeline_mode=pl.Buffered(1)),
            pl.BlockSpec((1, 1, d), lambda i, j, s: (s[i] * 6 + 5, 0, 0)),
            const((1, d)), const((1, d)),
        ],
        out_specs=pl.BlockSpec((tm, d), lambda i, j, s: (i, 0)),
        scratch_shapes=[pltpu.VMEM((tn, tm), F32), pltpu.VMEM((tn, tm), BF16), pltpu.VMEM((d, tm), F32)],
    )
    return pl.pallas_call(
        _peer_kernel,
        grid_spec=grid_spec,
        out_shape=jax.ShapeDtypeStruct((t, d), F32),
        compiler_params=pltpu.CompilerParams(
            dimension_semantics=("arbitrary", "arbitrary"), vmem_limit_bytes=VMEM_LIMIT),
        name="peer_dense",
    )(tile_seq, h2, u_bf, vt_bf, s2t, e2t, thrt, ct, x1, mod3, ln2_g, ln2_b)


def _seq_tables(seq_lens, unit):
    sid, first, last, pos = [], [], [], []
    for s, n in enumerate(seq_lens):
        k = n // unit
        sid += [s] * k
        first += [1] + [0] * (k - 1)
        last += [0] * (k - 1) + [1]
        pos += list(range(k))
    mk = lambda v: jnp.asarray(np.asarray(v, np.int32))
    return mk(sid), mk(first), mk(last), mk(pos)


def _pick(n, options):
    for o in options:
        if n % o == 0:
            return o
    raise ValueError(f"no tile size in {options} divides {n}")


def _reorder_w_in(w_in):
    pts = np.cumsum(IN_WIDTHS)[:-1]
    aq, ak, av, gq, gk, gv, lf, lb, gr, ga, gb = jnp.split(w_in, [int(p) for p in pts], axis=1)
    pad = jnp.zeros((w_in.shape[0], Z_WIDTH - Z_GLOW - 2 * GLA_GATE_RANK), w_in.dtype)
    return jnp.concatenate([aq, gv, gr, ga, gb, gq, gk, ak, av, lf, lb, pad], axis=1).astype(BF16)


def _layer(x, c_pad, seq_lens, w_ada, b_ada, w_in, attn_sink, gla_w2_fwd, gla_b_fwd, gla_w2_bwd, gla_b_bwd,
           gla_norm_w, w_branch_attn, w_branch_gla, w_out, ln1_g, ln1_b, peer_w_query, peer_subkeys,
           peer_u, peer_v, ln2_g, ln2_b):
    t, d = x.shape
    gcd_len = functools.reduce(math.gcd, seq_lens)
    tm_in = _pick(gcd_len, (512, 256, 128))
    tm_merge = _pick(gcd_len, (256, 128))
    tm_router = _pick(gcd_len, (256, 128))
    tm_peer = _pick(gcd_len, (512, 256, 128))
    chunk = 64

    mod = _ada(c_pad, w_ada, b_ada)
    mod3 = mod.reshape(c_pad.shape[0] * 6, 1, d)

    seq_in = _seq_tables(seq_lens, tm_in)[0]
    z = _ln_win(x, mod3, _reorder_w_in(w_in), seq_in, tm_in, Z_WIDTH // 7)

    _, blk_first, blk_last, blk_pos = _seq_tables(seq_lens, WINDOW_BLOCK)
    half = HEAD_DIM // 2
    inv = ROPE_THETA ** (-jnp.arange(half, dtype=F32) / half)
    ang = jnp.arange(max(seq_lens), dtype=F32)[:, None] * inv[None, :]
    cosf = jnp.concatenate([jnp.cos(ang), jnp.cos(ang)], axis=1)
    sinf = jnp.concatenate([-jnp.sin(ang), jnp.sin(ang)], axis=1)
    o_attn = _attention(z, attn_sink.astype(F32), cosf, sinf, blk_first, blk_last, blk_pos)

    _, ch_first, ch_last, _ = _seq_tables(seq_lens, chunk)
    o_f, o_b = _gla(z, gla_w2_fwd.astype(BF16), gla_b_fwd.reshape(1, GLA_K), gla_w2_bwd.astype(BF16),
                    gla_b_bwd.reshape(1, GLA_K), ch_first, ch_last, chunk, GLA_HEADS)

    seq_merge = _seq_tables(seq_lens, tm_merge)[0]
    row = lambda v: v.reshape(1, d)
    x1, h2 = _merge(o_attn, o_f, o_b, z, x, mod3, row(gla_norm_w), w_branch_attn.astype(BF16),
                    w_branch_gla.astype(BF16), w_out.astype(BF16), row(ln1_g), row(ln1_b), seq_merge, tm_merge)

    s2t, e2t, thrt, ct = _router(h2, peer_w_query.astype(BF16), peer_subkeys.astype(BF16), tm_router)

    seq_peer = _seq_tables(seq_lens, tm_peer)[0]
    return _peer(h2, peer_u.astype(BF16), peer_v.astype(BF16).T, s2t, e2t, thrt, ct, x1, mod3,
                 row(ln2_g), row(ln2_b), seq_peer, tm_peer, 1024)


def kernel(x_prompt, x_sample, c_prompt, c_sample, w_ada, b_ada, w_in, attn_sink, gla_w2_fwd, gla_b_fwd,
           gla_w2_bwd, gla_b_bwd, gla_norm_w, w_branch_attn, w_branch_gla, w_out, ln1_g, ln1_b, peer_w_query,
           peer_subkeys, peer_u, peer_v, ln2_g, ln2_b):
    assert w_ada.shape[0] == 1, "single-layer trunk"
    bp, sp, d = x_prompt.shape
    bs, ss, _ = x_sample.shape
    seq_lens = [sp] * bp + [ss] * bs
    x = jnp.concatenate([x_prompt.reshape(bp * sp, d), x_sample.reshape(bs * ss, d)], axis=0)
    nseq = bp + bs
    c_pad = jnp.zeros((-(-nseq // 8) * 8, d), F32).at[:nseq].set(jnp.concatenate([c_prompt, c_sample], axis=0))
    y = _layer(x, c_pad, seq_lens, w_ada[0], b_ada[0], w_in[0], attn_sink[0], gla_w2_fwd[0], gla_b_fwd[0],
               gla_w2_bwd[0], gla_b_bwd[0], gla_norm_w[0], w_branch_attn[0], w_branch_gla[0], w_out[0],
               ln1_g[0], ln1_b[0], peer_w_query[0], peer_subkeys[0], peer_u[0], peer_v[0], ln2_g[0], ln2_b[0])
    return (y[:bp * sp].reshape(bp, sp, d), y[bp * sp:].reshape(bs, ss, d))
```

```python
import functools
import math

import numpy as np
import jax
import jax.numpy as jnp
from jax import lax
from jax.experimental import pallas as pl
from jax.experimental.pallas import tpu as pltpu

F32 = jnp.float32
BF16 = jnp.bfloat16

D_MODEL = 2048
ATT_HEADS = 16
ATT_KV_HEADS = 4
ATT_GROUP = ATT_HEADS // ATT_KV_HEADS
HEAD_DIM = 128
WINDOW_BLOCK = 128
ROPE_THETA = 10000.0
GLA_HEADS = 4
GLA_DK = 256
GLA_DV = 512
GLA_GATE_RANK = 16
GLA_TAU = 16.0
PEER_HEADS = 8
PEER_NKEYS = 128
PEER_N = PEER_NKEYS * PEER_NKEYS
PEER_TOPK = 16
DN_ALPHA = 2.0 ** 0.25
LN_EPS = 1e-5
RMS_EPS = 1e-6

ATT_Q = ATT_HEADS * HEAD_DIM
ATT_KV = ATT_KV_HEADS * HEAD_DIM
GLA_K = GLA_HEADS * GLA_DK
GLA_V = GLA_HEADS * GLA_DV
IN_WIDTHS = (ATT_Q, ATT_KV, ATT_KV, GLA_K, GLA_K, GLA_V, GLA_GATE_RANK, GLA_GATE_RANK, GLA_V, D_MODEL, D_MODEL)

Z_AQ = 0
Z_GV = 2048
Z_GR = 4096
Z_GA = 6144
Z_GB = 8192
Z_GQ = 10240
Z_GK = 11264
Z_AK = 12288
Z_AV = 12800
Z_GLOW = 13312
Z_WIDTH = 13440

LANES = 128
PACK = 16
VMEM_LIMIT = 56 * 1024 * 1024

NEG_BIG = -1e30


def _ln(x):
    mu = jnp.mean(x, axis=-1, keepdims=True)
    xc = x - mu
    var = jnp.mean(xc * xc, axis=-1, keepdims=True)
    return xc * lax.rsqrt(var + LN_EPS)


def _dot_nt(a, b):
    return lax.dot_general(a, b, (((1,), (1,)), ((), ())), preferred_element_type=F32)


def _dot_tn(a, b):
    return lax.dot_general(a, b, (((0,), (0,)), ((), ())), preferred_element_type=F32)


def _dot(a, b):
    return jnp.dot(a, b, preferred_element_type=F32)


def _params(*sem):
    return pltpu.CompilerParams(dimension_semantics=sem, vmem_limit_bytes=VMEM_LIMIT)


def _ada_kernel(c_ref, w_ref, b_ref, o_ref):
    c = c_ref[...]
    a = (c * jax.nn.sigmoid(c)).astype(BF16)
    o_ref[...] = _dot(a, w_ref[...].astype(BF16)) + b_ref[...]


def _ada(c_pad, w_ada, b_ada, tn=1024):
    rows, d = c_pad.shape
    n = w_ada.shape[1]
    return pl.pallas_call(
        _ada_kernel,
        grid=(n // tn,),
        in_specs=[
            pl.BlockSpec((rows, d), lambda j: (0, 0)),
            pl.BlockSpec((d, tn), lambda j: (0, j)),
            pl.BlockSpec((1, tn), lambda j: (0, j)),
        ],
        out_specs=pl.BlockSpec((rows, tn), lambda j: (0, j)),
        out_shape=jax.ShapeDtypeStruct((rows, n), F32),
        compiler_params=_params("arbitrary"),
        name="ada_mod",
    )(c_pad, w_ada, b_ada.reshape(1, n))


def _two_group_specs(tm, d, n_prompt_tiles, mode=None):
    pm = lambda i, *_: (jnp.minimum(i, n_prompt_tiles - 1), 0)
    sm = lambda i, *_: (jnp.maximum(i - n_prompt_tiles, 0), 0)
    return pl.BlockSpec((tm, d), pm, pipeline_mode=mode), pl.BlockSpec((tm, d), sm, pipeline_mode=mode)


def _lnwin_kernel(seq_ref, xp_ref, xs_ref, sh_ref, sc_ref, w_ref, o_ref, h_ref, *, n_prompt_tiles):
    i = pl.program_id(0)

    def fill(x_ref):
        h_ref[...] = (_ln(x_ref[...]) * (1.0 + sc_ref[0]) + sh_ref[0]).astype(BF16)

    @pl.when((pl.program_id(1) == 0) & (i < n_prompt_tiles))
    def _():
        fill(xp_ref)

    @pl.when((pl.program_id(1) == 0) & (i >= n_prompt_tiles))
    def _():
        fill(xs_ref)

    o_ref[...] = _dot(h_ref[...], w_ref[...]).astype(o_ref.dtype)


def _ln_win(xp, xs, mod3, w_in_r, tile_seq, tm, tn):
    d = xp.shape[1]
    t = xp.shape[0] + xs.shape[0]
    npt = xp.shape[0] // tm
    n = w_in_r.shape[1]
    pspec, sspec = _two_group_specs(tm, d, npt)
    grid_spec = pltpu.PrefetchScalarGridSpec(
        num_scalar_prefetch=1,
        grid=(t // tm, n // tn),
        in_specs=[
            pspec, sspec,
            pl.BlockSpec((1, 1, d), lambda i, j, s: (s[i] * 6 + 0, 0, 0)),
            pl.BlockSpec((1, 1, d), lambda i, j, s: (s[i] * 6 + 1, 0, 0)),
            pl.BlockSpec((d, tn), lambda i, j, s: (0, j)),
        ],
        out_specs=pl.BlockSpec((tm, tn), lambda i, j, s: (i, j)),
        scratch_shapes=[pltpu.VMEM((tm, d), BF16)],
    )
    return pl.pallas_call(
        functools.partial(_lnwin_kernel, n_prompt_tiles=npt),
        grid_spec=grid_spec,
        out_shape=jax.ShapeDtypeStruct((t, n), BF16),
        compiler_params=_params("arbitrary", "arbitrary"),
        name="ln_win",
    )(tile_seq, xp, xs, mod3, mod3, w_in_r)


def _rope(t, cosf, sinf):
    return t * cosf + pltpu.roll(t, HEAD_DIM // 2, 1) * sinf


def _attn_kernel(first_ref, last_ref, pblk_ref, sink_ref, q_ref, kp_ref, kc_ref, kn_ref,
                 vp_ref, vc_ref, vn_ref, cp_ref, cc_ref, cn_ref, sp_ref, sc_ref, sn_ref, o_ref):
    n = pl.program_id(0)
    blk = WINDOW_BLOCK
    is_first = first_ref[n] == 1
    is_last = last_ref[n] == 1
    col_lo = jnp.where(is_first, blk, 0)
    col_hi = jnp.where(is_last, 2 * blk, 3 * blk)
    rows = ATT_GROUP * blk
    r = lax.broadcasted_iota(jnp.int32, (rows, 3 * blk), 0) & (blk - 1)
    c = lax.broadcasted_iota(jnp.int32, (rows, 3 * blk), 1)
    d = c - r
    valid = (d >= 0) & (d <= 2 * blk) & (c >= col_lo) & (c < col_hi)
    grp = lax.broadcasted_iota(jnp.int32, (rows, 1), 0) // blk
    cosc, sinc = cc_ref[...], sc_ref[...]
    scale = HEAD_DIM ** -0.5
    for g in range(ATT_KV_HEADS):
        ks = slice(g * HEAD_DIM, (g + 1) * HEAD_DIM)
        kband = jnp.concatenate([
            _rope(kp_ref[:, ks].astype(F32), cp_ref[...], sp_ref[...]).astype(BF16),
            _rope(kc_ref[:, ks].astype(F32), cosc, sinc).astype(BF16),
            _rope(kn_ref[:, ks].astype(F32), cn_ref[...], sn_ref[...]).astype(BF16)], axis=0)
        vband = jnp.concatenate([vp_ref[:, ks], vc_ref[:, ks], vn_ref[:, ks]], axis=0)
        qs = []
        sink_col = jnp.zeros((rows, 1), F32)
        for j in range(ATT_GROUP):
            h = g * ATT_GROUP + j
            qh = q_ref[:, h * HEAD_DIM:(h + 1) * HEAD_DIM].astype(F32)
            qs.append((_rope(qh, cosc, sinc) * scale).astype(BF16))
            sink_col = jnp.where(grp == j, sink_ref[h], sink_col)
        qg = jnp.concatenate(qs, axis=0)
        s = _dot_nt(qg, kband)
        s = jnp.where(valid, s, NEG_BIG)
        m = jnp.maximum(jnp.max(s, axis=-1, keepdims=True), sink_col)
        p = jnp.exp(s - m)
        den = jnp.sum(p, axis=-1, keepdims=True) + jnp.exp(sink_col - m)
        o = _dot(p.astype(BF16), vband) / den
        for j in range(ATT_GROUP):
            h = g * ATT_GROUP + j
            o_ref[:, h * HEAD_DIM:(h + 1) * HEAD_DIM] = o[j * blk:(j + 1) * blk].astype(o_ref.dtype)


def _attention(z, sink, cosf, sinf, blk_first, blk_last, blk_pos):
    t = z.shape[0]
    blk = WINDOW_BLOCK
    nblk = t // blk
    npos = cosf.shape[0] // blk
    kcol = Z_AK // ATT_KV
    vcol = Z_AV // ATT_KV
    prev = lambda n: jnp.maximum(n - 1, 0)
    nxt = lambda n: jnp.minimum(n + 1, nblk - 1)
    pprev = lambda p: jnp.maximum(p - 1, 0)
    pnxt = lambda p: jnp.minimum(p + 1, npos - 1)
    tab = lambda f: pl.BlockSpec((blk, HEAD_DIM), lambda n, a, b, p: (f(p[n]), 0))
    same = lambda p: p
    grid_spec = pltpu.PrefetchScalarGridSpec(
        num_scalar_prefetch=3,
        grid=(nblk,),
        in_specs=[
            pl.BlockSpec(memory_space=pltpu.SMEM),
            pl.BlockSpec((blk, ATT_Q), lambda n, a, b, p: (n, Z_AQ // ATT_Q)),
            pl.BlockSpec((blk, ATT_KV), lambda n, a, b, p: (prev(n), kcol)),
            pl.BlockSpec((blk, ATT_KV), lambda n, a, b, p: (n, kcol)),
            pl.BlockSpec((blk, ATT_KV), lambda n, a, b, p: (nxt(n), kcol)),
            pl.BlockSpec((blk, ATT_KV), lambda n, a, b, p: (prev(n), vcol)),
            pl.BlockSpec((blk, ATT_KV), lambda n, a, b, p: (n, vcol)),
            pl.BlockSpec((blk, ATT_KV), lambda n, a, b, p: (nxt(n), vcol)),
            tab(pprev), tab(same), tab(pnxt),
            tab(pprev), tab(same), tab(pnxt),
        ],
        out_specs=pl.BlockSpec((blk, ATT_Q), lambda n, a, b, p: (n, 0)),
    )
    return pl.pallas_call(
        _attn_kernel,
        grid_spec=grid_spec,
        out_shape=jax.ShapeDtypeStruct((t, ATT_Q), BF16),
        compiler_params=_params("arbitrary"),
        name="win_attn",
    )(blk_first, blk_last, blk_pos, sink, z, z, z, z, z, z, z, cosf, cosf, cosf, sinf, sinf, sinf)


GLA_CHUNK = 128
GLA_SUB = 16
GLA_NSUB = GLA_CHUNK // GLA_SUB


def _logsig(z):
    return -(jnp.maximum(-z, 0.0) + jnp.log1p(jnp.exp(-jnp.abs(z))))


def _mask_sums(masks, g):
    g_hi = g.astype(BF16)
    g_lo = (g - g_hi.astype(F32)).astype(BF16)
    return _dot(masks, g_hi) + _dot(masks, g_lo)


def _gla_intra_kernel(q_ref, k_ref, v_ref, l_ref, w2f_ref, bf_ref, w2b_ref, bb_ref,
                      qtf_ref, kdf_ref, qtb_ref, kdb_ref, decf_ref, decb_ref, oi_ref):
    c, sub, nsub = GLA_CHUNK, GLA_SUB, GLA_NSUB
    one = lambda m: jnp.where(m, 1.0, 0.0).astype(BF16)
    ri = lax.broadcasted_iota(jnp.int32, (c, c), 0)
    ci = lax.broadcasted_iota(jnp.int32, (c, c), 1)
    blk0 = (ri // sub) * sub
    bi = lax.broadcasted_iota(jnp.int32, (PACK, c), 0) * sub
    cj = lax.broadcasted_iota(jnp.int32, (PACK, c), 1)
    masks_f = jnp.concatenate([one(ci <= ri), one((ci > blk0) & (ci <= ri)), one(cj <= bi)], axis=0)
    masks_b = jnp.concatenate(
        [one(ci >= ri), one((ci >= ri) & (ci < blk0 + sub - 1)), one(cj >= bi + sub - 1)], axis=0)

    glow = l_ref[...]
    gf = _logsig(_dot(glow[:, 0:GLA_GATE_RANK], w2f_ref[...]) + bf_ref[...]) * (1.0 / GLA_TAU)
    gb = _logsig(_dot(glow[:, GLA_GATE_RANK:2 * GLA_GATE_RANK], w2b_ref[...]) + bb_ref[...]) * (1.0 / GLA_TAU)
    sums_f = _mask_sums(masks_f, gf)
    sums_b = _mask_sums(masks_b, gb)
    q = q_ref[...].astype(F32) * (GLA_DK ** -0.5)
    k = k_ref[...].astype(F32)

    rr = lax.broadcasted_iota(jnp.int32, (sub, c), 0)
    cc = lax.broadcasted_iota(jnp.int32, (sub, c), 1)
    zero_blk = jnp.zeros((sub, GLA_DK), BF16)

    def direction(sums, reverse, qt_ref, kd_ref, dec_ref):
        b, win, ref = sums[0:c], sums[c:2 * c], sums[2 * c:2 * c + PACK]
        edge = b[0:1] if reverse else b[c - 1:c]
        qt_ref[...] = (q * jnp.exp(b)).astype(BF16)
        kd_ref[...] = (k * jnp.exp(edge - b)).astype(BF16)
        dec_ref[0] = jnp.exp(edge)
        qw = (q * jnp.exp(win)).astype(BF16)
        kw = k * jnp.exp(-win)
        att = [[] for _ in range(GLA_HEADS)]
        for i in range(nsub):
            dmat = jnp.exp(jnp.minimum(ref[i:i + 1] - ref, 0.0))
            live = range(i, nsub) if reverse else range(0, i + 1)
            for hd in range(GLA_HEADS):
                sl = slice(hd * GLA_DK, (hd + 1) * GLA_DK)
                kh = jnp.concatenate(
                    [(kw[j * sub:(j + 1) * sub, sl] * dmat[j:j + 1, sl]).astype(BF16) if j in live else zero_blk
                     for j in range(nsub)], axis=0)
                a = _dot_nt(qw[i * sub:(i + 1) * sub, sl], kh)
                keep = (cc >= rr + i * sub) if reverse else (cc <= rr + i * sub)
                att[hd].append(jnp.where(keep, a, 0.0))
        return [jnp.concatenate(rows, axis=0) for rows in att]

    att_f = direction(sums_f, False, qtf_ref, kdf_ref, decf_ref)
    att_b = direction(sums_b, True, qtb_ref, kdb_ref, decb_ref)
    for hd in range(GLA_HEADS):
        vsl = slice(hd * GLA_DV, (hd + 1) * GLA_DV)
        oi_ref[:, vsl] = _dot((att_f[hd] + att_b[hd]).astype(BF16), v_ref[:, vsl]).astype(oi_ref.dtype)


def _gla_intra(z, w2f, bf, w2b, bb):
    t = z.shape[0]
    c = GLA_CHUNK
    nch = t // c
    zspec = lambda w, off: pl.BlockSpec((c, w), lambda n: (n, off // w))
    wspec = pl.BlockSpec((GLA_GATE_RANK, GLA_K), lambda n: (0, 0))
    bspec = pl.BlockSpec((1, GLA_K), lambda n: (0, 0))
    tok = lambda w: pl.BlockSpec((c, w), lambda n: (n, 0))
    dec = pl.BlockSpec((1, 1, GLA_K), lambda n: (n, 0, 0))
    tk = jax.ShapeDtypeStruct((t, GLA_K), BF16)
    dk = jax.ShapeDtypeStruct((nch, 1, GLA_K), F32)
    return pl.pallas_call(
        _gla_intra_kernel,
        grid=(nch,),
        in_specs=[zspec(GLA_K, Z_GQ), zspec(GLA_K, Z_GK), zspec(GLA_V, Z_GV), zspec(LANES, Z_GLOW),
                  wspec, bspec, wspec, bspec],
        out_specs=[tok(GLA_K), tok(GLA_K), tok(GLA_K), tok(GLA_K), dec, dec, tok(GLA_V)],
        out_shape=[tk, tk, tk, tk, dk, dk, jax.ShapeDtypeStruct((t, GLA_V), BF16)],
        compiler_params=_params("arbitrary"),
        name="gla_intra",
    )(z, z, z, z, w2f, bf, w2b, bb)


def _gla_state_kernel(first_ref, last_ref, qf_ref, kf_ref, vf_ref, df_ref, qb_ref, kb_ref, vb_ref, db_ref,
                      of_ref, ob_ref, sf_ref, sb_ref):
    n = pl.program_id(0)
    nch = pl.num_programs(0)

    @pl.when(first_ref[n] == 1)
    def _():
        sf_ref[...] = jnp.zeros_like(sf_ref)

    @pl.when(last_ref[nch - 1 - n] == 1)
    def _():
        sb_ref[...] = jnp.zeros_like(sb_ref)

    for hd in range(GLA_HEADS):
        ksl = slice(hd * GLA_DK, (hd + 1) * GLA_DK)
        vsl = slice(hd * GLA_DV, (hd + 1) * GLA_DV)
        for q_ref, k_ref, v_ref, d_ref, o_ref, s_ref in (
                (qf_ref, kf_ref, vf_ref, df_ref, of_ref, sf_ref), (qb_ref, kb_ref, vb_ref, db_ref, ob_ref, sb_ref)):
            state = s_ref[hd]
            o_ref[:, vsl] = _dot_nt(q_ref[:, ksl], state.astype(BF16)).astype(o_ref.dtype)
            s_ref[hd] = d_ref[0][:, ksl] * state + _dot_tn(v_ref[:, vsl], k_ref[:, ksl])


def _gla_state(z, qtf, kdf, decf, qtb, kdb, decb, ch_first, ch_last):
    t = z.shape[0]
    c = GLA_CHUNK
    nch = t // c
    fwd = lambda n: n
    bwd = lambda n: nch - 1 - n
    tok = lambda f: pl.BlockSpec((c, GLA_K), lambda n, a, b: (f(n), 0))
    val = lambda f: pl.BlockSpec((c, GLA_V), lambda n, a, b: (f(n), Z_GV // GLA_V))
    dec = lambda f: pl.BlockSpec((1, 1, GLA_K), lambda n, a, b: (f(n), 0, 0))
    out = lambda f: pl.BlockSpec((c, GLA_V), lambda n, a, b: (f(n), 0))
    grid_spec = pltpu.PrefetchScalarGridSpec(
        num_scalar_prefetch=2,
        grid=(nch,),
        in_specs=[tok(fwd), tok(fwd), val(fwd), dec(fwd), tok(bwd), tok(bwd), val(bwd), dec(bwd)],
        out_specs=[out(fwd), out(bwd)],
        scratch_shapes=[pltpu.VMEM((GLA_HEADS, GLA_DV, GLA_DK), F32), pltpu.VMEM((GLA_HEADS, GLA_DV, GLA_DK), F32)],
    )
    o = jax.ShapeDtypeStruct((t, GLA_V), BF16)
    return pl.pallas_call(
        _gla_state_kernel,
        grid_spec=grid_spec,
        out_shape=[o, o],
        compiler_params=_params("arbitrary"),
        name="gla_state",
    )(ch_first, ch_last, qtf, kdf, z, decf, qtb, kdb, z, decb)


def _merge_kernel(seq_ref, oa_ref, oi_ref, of_ref, ob_ref, gr_ref, ga_ref, gb_ref, xp_ref, xs_ref, nw_ref,
                  wa_ref, wb_ref, wo_ref, g1_ref, l1g_ref, l1b_ref, sh2_ref, sc2_ref, x1_ref, h2_ref,
                  *, n_prompt_tiles):
    i = pl.program_id(0)
    og =oi_ref[...].astype(F32) + of_ref[...].astype(F32) + ob_ref[...].astype(F32)
    parts = []
    for h in range(GLA_HEADS):
        th = og[:, h * GLA_DV:(h + 1) * GLA_DV]
        ms = jnp.mean(th * th, axis=-1, keepdims=True)
        parts.append(th * lax.rsqrt(ms + RMS_EPS))
    gr = gr_ref[...].astype(F32)
    ogn = jnp.concatenate(parts, axis=1) * nw_ref[...] * (gr * jax.nn.sigmoid(gr))
    a = _dot(oa_ref[...], wa_ref[...])
    b = _dot(ogn.astype(BF16), wb_ref[...])
    merged = jax.nn.sigmoid(ga_ref[...].astype(F32)) * a + jax.nn.sigmoid(gb_ref[...].astype(F32)) * b
    mix = g1_ref[0] * _dot(merged.astype(BF16), wo_ref[...])

    def finish(x_ref):
        x1 = _ln(DN_ALPHA * x_ref[...] + mix) * l1g_ref[...] + l1b_ref[...]
        x1_ref[...] = x1
        h2_ref[...] = (_ln(x1) * (1.0 + sc2_ref[0]) + sh2_ref[0]).astype(BF16)

    @pl.when(i < n_prompt_tiles)
    def _():
        finish(xp_ref)

    @pl.when(i >= n_prompt_tiles)
    def _():
        finish(xs_ref)


def _merge(o_attn, o_intra, o_f, o_b, z, xp, xs, mod3, norm_w, wa, wb, wo, ln1_g, ln1_b, tile_seq, tm):
    d = xp.shape[1]
    t = xp.shape[0] + xs.shape[0]
    npt = xp.shape[0] // tm
    tok = lambda col: pl.BlockSpec((tm, d), lambda i, s: (i, col))
    const = lambda shape: pl.BlockSpec(shape, lambda i, s: (0,) * len(shape), pipeline_mode=pl.Buffered(1))
    modspec = lambda k: pl.BlockSpec((1, 1, d), lambda i, s: (s[i] * 6 + k, 0, 0))
    pspec, sspec = _two_group_specs(tm, d, npt, pl.Buffered(1))
    grid_spec = pltpu.PrefetchScalarGridSpec(
        num_scalar_prefetch=1,
        grid=(t // tm,),
        in_specs=[tok(0), tok(0), tok(0), tok(0), tok(Z_GR // d), tok(Z_GA // d), tok(Z_GB // d), pspec, sspec,
                  const((1, d)), const((d, d)), const((d, d)), const((d, d)),
                  modspec(2), const((1, d)), const((1, d)), modspec(3), modspec(4)],
        out_specs=[tok(0), tok(0)],
    )
    return pl.pallas_call(
        functools.partial(_merge_kernel, n_prompt_tiles=npt),
        grid_spec=grid_spec,
        out_shape=[jax.ShapeDtypeStruct((t, d), F32), jax.ShapeDtypeStruct((t, d), BF16)],
        compiler_params=_params("arbitrary"),
        name="merge_ln1",
    )(tile_seq, o_attn, o_intra, o_f, o_b, z, z, z, xp, xs, norm_w, wa, wb, wo, mod3, ln1_g, ln1_b, mod3, mod3)


PEER_EXTRACT = PEER_TOPK + 1
SLAB = 8
RANK_NONE = 127.0


def _slabs(s):
    return [s[i * SLAB:(i + 1) * SLAB] for i in range(s.shape[0] // SLAB)]


def _extract_desc(slabs, count, want_rank=False):
    vals = []
    ranks = [jnp.full(sl.shape, RANK_NONE, F32) for sl in slabs] if want_rank else None
    for it in range(count):
        m8 = functools.reduce(jnp.maximum, slabs)
        m = jnp.max(m8, axis=0, keepdims=True)
        vals.append(m)
        hit = [sl == m for sl in slabs]
        if want_rank:
            ranks = [jnp.where(h, float(it), r) for h, r in zip(hit, ranks)]
        slabs = [jnp.where(h, -jnp.inf, sl) for h, sl in zip(hit, slabs)]
    return (vals, ranks) if want_rank else vals


def _stack_rows(vals, rows, lanes):
    ridx = lax.broadcasted_iota(jnp.int32, (rows, lanes), 0)
    acc = jnp.full((rows, lanes), -jnp.inf, F32)
    for i, v in enumerate(vals):
        acc = jnp.where(ridx == i, v, acc)
    return acc


def _router_kernel(h2_ref, wq_ref, sk_ref, rk_ref, e2_ref, cnt_ref, c_ref):
    tm = h2_ref.shape[0]
    kx = PEER_EXTRACT
    q = _dot(h2_ref[...], wq_ref[...]).astype(BF16)
    ridx = lax.broadcasted_iota(jnp.int32, (SLAB, tm), 0)
    for h in range(PEER_HEADS):
        s1 = _dot_nt(sk_ref[h, 0], q[:, (2 * h) * PEER_NKEYS:(2 * h + 1) * PEER_NKEYS])
        s2 = _dot_nt(sk_ref[h, 1], q[:, (2 * h + 1) * PEER_NKEYS:(2 * h + 2) * PEER_NKEYS])
        v1 = _extract_desc(_slabs(s1), kx)
        v2, rank2 = _extract_desc(_slabs(s2), kx, want_rank=True)
        st1 = _stack_rows(v1, 3 * SLAB, tm)
        st2 = _stack_rows(v2, 3 * SLAB, tm)
        cand = []
        for a in range(SLAB):
            nb = kx // (a + 1)
            for sl in range((nb + SLAB - 1) // SLAB):
                piece = st2[sl * SLAB:(sl + 1) * SLAB] + v1[a]
                if nb < (sl + 1) * SLAB:
                    piece = jnp.where(ridx < nb - sl * SLAB, piece, -jnp.inf)
                cand.append(piece)
        cand.append(st1[SLAB:2 * SLAB] + v2[0])
        cand.append(jnp.where(ridx < kx - 2 * SLAB, st1[2 * SLAB:3 * SLAB] + v2[0], -jnp.inf))
        top = _extract_desc(cand, kx)
        zsum = functools.reduce(lambda acc, v: acc + jnp.exp(v - top[0]), top[:PEER_TOPK], jnp.zeros_like(top[0]))
        tau = 0.5 * (top[PEER_TOPK - 1] + top[PEER_TOPK])
        cnt = jnp.zeros_like(s1)
        for b in range(PEER_TOPK):
            cnt = cnt + jnp.where(s1 + v2[b] >= tau, 1.0, 0.0)
        rk_ref[h] = jnp.concatenate(rank2, axis=0).astype(BF16)
        e2_ref[h] = jnp.exp(s2 - v2[0]).astype(BF16)
        cnt_ref[h] = cnt
        c_ref[h] = jnp.exp(s1 - v1[0]) / zsum


def _router(h2, wq, subkeys, tm):
    t, d = h2.shape
    nq = wq.shape[1]
    shape = (PEER_HEADS, PEER_NKEYS, t)
    ospec = pl.BlockSpec((PEER_HEADS, PEER_NKEYS, tm), lambda i: (0, 0, i))
    return pl.pallas_call(
        _router_kernel,
        grid=(t // tm,),
        in_specs=[
            pl.BlockSpec((tm, d), lambda i: (i, 0)),
            pl.BlockSpec((d, nq), lambda i: (0, 0), pipeline_mode=pl.Buffered(1)),
            pl.BlockSpec(subkeys.shape, lambda i: (0, 0, 0, 0), pipeline_mode=pl.Buffered(1)),
        ],
        out_specs=[ospec, ospec, ospec, ospec],
        out_shape=[jax.ShapeDtypeStruct(shape, BF16), jax.ShapeDtypeStruct(shape, BF16),
                   jax.ShapeDtypeStruct(shape, F32), jax.ShapeDtypeStruct(shape, F32)],
        compiler_params=_params("arbitrary"),
        name="peer_router",
    )(h2, wq, subkeys)


_SQRT_HALF = math.sqrt(0.5)


def _gelu(x):
    return 0.5 * x * (1.0 + lax.erf(x * _SQRT_HALF))


def _peer_first_kernel(u_ref, h2_ref, o_ref):
    o_ref[...] = _dot_nt(u_ref[...], h2_ref[...])


def _peer_first(h2, u_bf, tm, tn):
    d = h2.shape[1]
    return pl.pallas_call(
        _peer_first_kernel,
        grid=(1,),
        in_specs=[pl.BlockSpec((tn, d), lambda i: (0, 0)), pl.BlockSpec((tm, d), lambda i: (0, 0))],
        out_specs=pl.BlockSpec((tn, tm), lambda i: (0, 0)),
        out_shape=jax.ShapeDtypeStruct((tn, tm), F32),
        compiler_params=_params("arbitrary"),
        name="peer_first",
    )(u_bf, h2)


def _peer_kernel(seq_ref, h2n_ref, un_ref, vt_ref, rk_ref, e2_ref, cnt_ref, c_ref, at_init_ref, x1_ref, g2_ref,
                 l2g_ref, l2b_ref, op_ref, os_ref, at0_ref, at1_ref, wt_ref, acc_ref, *, n_prompt_tiles):
    i = pl.program_id(0)
    j = pl.program_id(1)
    tm = h2n_ref.shape[0]
    ni = un_ref.shape[0] // PEER_NKEYS
    groups = PEER_NKEYS // PACK

    @pl.when((i == 0) & (j == 0))
    def _():
        at0_ref[...] = at_init_ref[...]

    @pl.when(j == 0)
    def _():
        acc_ref[...] = jnp.zeros_like(acc_ref)

    def step(at_cur, at_next):
        at_next[...] = _dot_nt(un_ref[...], h2n_ref[...])
        for ii in range(ni):
            rows = slice(ii * PEER_NKEYS, (ii + 1) * PEER_NKEYS)
            gate = None
            for h in range(PEER_HEADS):
                cnt = jnp.broadcast_to(cnt_ref[h, ii:ii + 1, :], (PACK, tm)).astype(BF16)
                coef = jnp.broadcast_to(c_ref[h, ii:ii + 1, :], (PACK, tm)).astype(BF16)
                sel = jnp.where(rk_ref[h] < cnt[None], e2_ref[h] * coef[None], jnp.zeros((), BF16))
                gate = sel if gate is None else gate + sel
            act = _gelu(at_cur[rows, :]).astype(BF16)
            wt_ref[rows, :] = gate.reshape(PEER_NKEYS, tm) * act
        acc_ref[...] += _dot(vt_ref[...], wt_ref[...])

    @pl.when(j % 2 == 0)
    def _():
        step(at0_ref, at1_ref)

    @pl.when(j % 2 == 1)
    def _():
        step(at1_ref, at0_ref)

    @pl.when(j == pl.num_programs(1) - 1)
    def _():
        ff = acc_ref[...].T
        y = _ln(DN_ALPHA * x1_ref[...] + g2_ref[0] * ff) * l2g_ref[...] + l2b_ref[...]

        @pl.when(i < n_prompt_tiles)
        def _():
            op_ref[...] = y

        @pl.when(i >= n_prompt_tiles)
        def _():
            os_ref[...] = y


def _peer(h2, u_bf, vt_bf, rk, e2, cnt, coef, x1, mod3, ln2_g, ln2_b, tile_seq, tm, tn, t_prompt):
    t, d = x1.shape
    ne = u_bf.shape[0]
    ni = tn // PEER_NKEYS
    nti, ntj = t // tm, ne // tn
    assert ntj % 2 == 0, "the two A^T buffers alternate by the parity of the expert-tile index"
    npt = t_prompt // tm
    groups = PEER_NKEYS // PACK
    rk4 = rk.reshape(PEER_HEADS, groups, PACK, t)
    e24 = e2.reshape(PEER_HEADS, groups, PACK, t)
    at_init = _peer_first(h2, u_bf, tm, tn)
    nxt_i = lambda i, j: jnp.minimum(jnp.where(j == ntj - 1, i + 1, i), nti - 1)
    nxt_j = lambda j: jnp.where(j == ntj - 1, 0, j + 1)
    once = pl.Buffered(1)
    const = lambda shape: pl.BlockSpec(shape, lambda i, j, s: (0,) * len(shape), pipeline_mode=once)
    grid_spec = pltpu.PrefetchScalarGridSpec(
        num_scalar_prefetch=1,
        grid=(nti, ntj),
        in_specs=[
            pl.BlockSpec((tm, d), lambda i, j, s: (nxt_i(i, j), 0), pipeline_mode=once),
            pl.BlockSpec((tn, d), lambda i, j, s: (nxt_j(j), 0)),
            pl.BlockSpec((d, tn), lambda i, j, s: (0, j)),
            pl.BlockSpec((PEER_HEADS, groups, PACK, tm), lambda i, j, s: (0, 0, 0, i), pipeline_mode=once),
            pl.BlockSpec((PEER_HEADS, groups, PACK, tm), lambda i, j, s: (0, 0, 0, i), pipeline_mode=once),
            pl.BlockSpec((PEER_HEADS, ni, tm), lambda i, j, s: (0, j, i)),
            pl.BlockSpec((PEER_HEADS, ni, tm), lambda i, j, s: (0, j, i)),
            const((tn, tm)),
            pl.BlockSpec((tm, d), lambda i, j, s: (i, 0), pipeline_mode=once),
            pl.BlockSpec((1, 1, d), lambda i, j, s: (s[i] * 6 + 5, 0, 0)),
            const((1, d)), const((1, d)),
        ],
        out_specs=[pl.BlockSpec((tm, d), lambda i, j, s: (jnp.minimum(i, npt - 1), 0)),
                   pl.BlockSpec((tm, d), lambda i, j, s: (jnp.maximum(i - npt, 0), 0))],
        scratch_shapes=[pltpu.VMEM((tn, tm), F32), pltpu.VMEM((tn, tm), F32), pltpu.VMEM((tn, tm), BF16),
                        pltpu.VMEM((d, tm), F32)],
    )
    return pl.pallas_call(
        functools.partial(_peer_kernel, n_prompt_tiles=npt),
        grid_spec=grid_spec,
        out_shape=[jax.ShapeDtypeStruct((t_prompt, d), F32), jax.ShapeDtypeStruct((t - t_prompt, d), F32)],
        compiler_params=_params("arbitrary", "arbitrary"),
        name="peer_dense",
    )(tile_seq, h2, u_bf, vt_bf, rk4, e24, cnt, coef, at_init, x1, mod3, ln2_g, ln2_b)


def _seq_tables(seq_lens, unit):
    sid, first, last, pos = [], [], [], []
    for s, n in enumerate(seq_lens):
        k = n // unit
        sid += [s] * k
        first += [1] + [0] * (k - 1)
        last += [0] * (k - 1) + [1]
        pos += list(range(k))
    mk = lambda v: jnp.asarray(np.asarray(v, np.int32))
    return mk(sid), mk(first), mk(last), mk(pos)


def _pick(n, options):
    for o in options:
        if n % o == 0:
            return o
    raise ValueError(f"no tile size in {options} divides {n}")


def _reorder_w_in(w_in):
    pts = np.cumsum(IN_WIDTHS)[:-1]
    aq, ak, av, gq, gk, gv, lf, lb, gr, ga, gb = jnp.split(w_in, [int(p) for p in pts], axis=1)
    pad = jnp.zeros((w_in.shape[0], Z_WIDTH - Z_GLOW - 2 * GLA_GATE_RANK), w_in.dtype)
    return jnp.concatenate([aq, gv, gr, ga, gb, gq, gk, ak, av, lf, lb, pad], axis=1).astype(BF16)


def _layer(xp, xs, c_pad, seq_lens, w_ada, b_ada, w_in, attn_sink, gla_w2_fwd, gla_b_fwd, gla_w2_bwd, gla_b_bwd,
           gla_norm_w, w_branch_attn, w_branch_gla, w_out, ln1_g, ln1_b, peer_w_query, peer_subkeys,
           peer_u, peer_v, ln2_g, ln2_b):
    d = xp.shape[1]
    gcd_len = functools.reduce(math.gcd, seq_lens)
    tm_in = _pick(gcd_len, (512, 256, 128))
    tm_merge = _pick(gcd_len, (256, 128))
    tm_router = _pick(gcd_len, (256, 128))
    tm_peer = _pick(gcd_len, (512, 256, 128))

    mod = _ada(c_pad, w_ada, b_ada)
    mod3 = mod.reshape(c_pad.shape[0] * 6, 1, d)

    seq_in = _seq_tables(seq_lens, tm_in)[0]
    z = _ln_win(xp, xs, mod3, _reorder_w_in(w_in), seq_in, tm_in, Z_WIDTH // 7)

    _, blk_first, blk_last, blk_pos = _seq_tables(seq_lens, WINDOW_BLOCK)
    half = HEAD_DIM // 2
    inv = ROPE_THETA ** (-jnp.arange(half, dtype=F32) / half)
    ang = jnp.arange(max(seq_lens), dtype=F32)[:, None] * inv[None, :]
    cosf = jnp.concatenate([jnp.cos(ang), jnp.cos(ang)], axis=1)
    sinf = jnp.concatenate([-jnp.sin(ang), jnp.sin(ang)], axis=1)
    o_attn = _attention(z, attn_sink.astype(F32), cosf, sinf, blk_first, blk_last, blk_pos)

    _, ch_first, ch_last, _ = _seq_tables(seq_lens, GLA_CHUNK)
    qtf, kdf, qtb, kdb, decf, decb, o_intra = _gla_intra(
        z, gla_w2_fwd.astype(BF16), gla_b_fwd.reshape(1, GLA_K), gla_w2_bwd.astype(BF16), gla_b_bwd.reshape(1, GLA_K))
    o_f, o_b = _gla_state(z, qtf, kdf, decf, qtb, kdb, decb, ch_first, ch_last)

    seq_merge = _seq_tables(seq_lens, tm_merge)[0]
    row = lambda v: v.reshape(1, d)
    x1, h2 = _merge(o_attn, o_intra, o_f, o_b, z, xp, xs, mod3, row(gla_norm_w), w_branch_attn.astype(BF16),
                    w_branch_gla.astype(BF16), w_out.astype(BF16), row(ln1_g), row(ln1_b), seq_merge, tm_merge)

    rk, e2, cnt, coef = _router(h2, peer_w_query.astype(BF16), peer_subkeys.astype(BF16), tm_router)

    seq_peer = _seq_tables(seq_lens, tm_peer)[0]
    return _peer(h2, peer_u.astype(BF16), peer_v.astype(BF16).T, rk, e2, cnt, coef, x1, mod3,
                 row(ln2_g), row(ln2_b), seq_peer, tm_peer, 1024, xp.shape[0])


def kernel(x_prompt, x_sample, c_prompt, c_sample, w_ada, b_ada, w_in, attn_sink, gla_w2_fwd, gla_b_fwd,
           gla_w2_bwd, gla_b_bwd, gla_norm_w, w_branch_attn, w_branch_gla, w_out, ln1_g, ln1_b, peer_w_query,
           peer_subkeys, peer_u, peer_v, ln2_g, ln2_b):
    assert w_ada.shape[0] == 1, "single-layer trunk"
    bp, sp, d = x_prompt.shape
    bs, ss, _ = x_sample.shape
    seq_lens = [sp] * bp + [ss] * bs
    nseq = bp + bs
    c_pad = jnp.zeros((-(-nseq // 8) * 8, d), F32).at[:nseq].set(jnp.concatenate([c_prompt, c_sample], axis=0))
    yp, ys = _layer(x_prompt.reshape(bp * sp, d), x_sample.reshape(bs * ss, d), c_pad, seq_lens, w_ada[0], b_ada[0],
                    w_in[0], attn_sink[0], gla_w2_fwd[0], gla_b_fwd[0], gla_w2_bwd[0], gla_b_bwd[0], gla_norm_w[0],
                    w_branch_attn[0], w_branch_gla[0], w_out[0], ln1_g[0], ln1_b[0], peer_w_query[0],
                    peer_subkeys[0], peer_u[0], peer_v[0], ln2_g[0], ln2_b[0])
    return (yp.reshape(bp, sp, d), ys.reshape(bs, ss, d))
```

```python
import functools
import math

import numpy as np
import jax
import jax.numpy as jnp
from jax import lax
from jax.experimental import pallas as pl
from jax.experimental.pallas import tpu as pltpu

F32 = jnp.float32
BF16 = jnp.bfloat16

D_MODEL = 2048
ATT_HEADS = 16
ATT_KV_HEADS = 4
ATT_GROUP = ATT_HEADS // ATT_KV_HEADS
HEAD_DIM = 128
WINDOW_BLOCK = 128
ROPE_THETA = 10000.0
GLA_HEADS = 4
GLA_DK = 256
GLA_DV = 512
GLA_GATE_RANK = 16
GLA_TAU = 16.0
PEER_HEADS = 8
PEER_NKEYS = 128
PEER_N = PEER_NKEYS * PEER_NKEYS
PEER_TOPK = 16
DN_ALPHA = 2.0 ** 0.25
LN_EPS = 1e-5
RMS_EPS = 1e-6

ATT_Q = ATT_HEADS * HEAD_DIM
ATT_KV = ATT_KV_HEADS * HEAD_DIM
GLA_K = GLA_HEADS * GLA_DK
GLA_V = GLA_HEADS * GLA_DV
IN_WIDTHS = (ATT_Q, ATT_KV, ATT_KV, GLA_K, GLA_K, GLA_V, GLA_GATE_RANK, GLA_GATE_RANK, GLA_V, D_MODEL, D_MODEL)

Z_AQ = 0
Z_GV = 2048
Z_GR = 4096
Z_GA = 6144
Z_GB = 8192
Z_GQ = 10240
Z_GK = 11264
Z_AK = 12288
Z_AV = 12800
Z_GLOW = 13312
Z_WIDTH = 13440

LANES = 128
PACK = 16
VMEM_LIMIT = 56 * 1024 * 1024

NEG_BIG = -1e30


def _ln(x):
    mu = jnp.mean(x, axis=-1, keepdims=True)
    xc = x - mu
    var = jnp.mean(xc * xc, axis=-1, keepdims=True)
    return xc * lax.rsqrt(var + LN_EPS)


def _dot_nt(a, b):
    return lax.dot_general(a, b, (((1,), (1,)), ((), ())), preferred_element_type=F32)


def _dot_tn(a, b):
    return lax.dot_general(a, b, (((0,), (0,)), ((), ())), preferred_element_type=F32)


def _dot(a, b):
    return jnp.dot(a, b, preferred_element_type=F32)


def _params(*sem, flags=None, vmem=VMEM_LIMIT):
    return pltpu.CompilerParams(dimension_semantics=sem, vmem_limit_bytes=vmem, flags=flags)


def _ada_kernel(c_ref, w_ref, b_ref, o_ref):
    c = c_ref[...]
    a = (c * jax.nn.sigmoid(c)).astype(BF16)
    o_ref[...] = _dot(a, w_ref[...].astype(BF16)) + b_ref[...]


def _ada(c_pad, w_ada, b_ada, tn=1024):
    rows, d = c_pad.shape
    n = w_ada.shape[1]
    return pl.pallas_call(
        _ada_kernel,
        grid=(n // tn,),
        in_specs=[
            pl.BlockSpec((rows, d), lambda j: (0, 0)),
            pl.BlockSpec((d, tn), lambda j: (0, j)),
            pl.BlockSpec((1, tn), lambda j: (0, j)),
        ],
        out_specs=pl.BlockSpec((rows, tn), lambda j: (0, j)),
        out_shape=jax.ShapeDtypeStruct((rows, n), F32),
        compiler_params=_params("arbitrary"),
        name="ada_mod",
    )(c_pad, w_ada, b_ada.reshape(1, n))


def _two_group_specs(tm, d, n_prompt_tiles, mode=None):
    pm = lambda i, *_: (jnp.minimum(i, n_prompt_tiles - 1), 0)
    sm = lambda i, *_: (jnp.maximum(i - n_prompt_tiles, 0), 0)
    return pl.BlockSpec((tm, d), pm, pipeline_mode=mode), pl.BlockSpec((tm, d), sm, pipeline_mode=mode)


def _lnwin_kernel(seq_ref, xp_ref, xs_ref, sh_ref, sc_ref, w_ref, o_ref, h_ref, *, n_prompt_tiles):
    i = pl.program_id(0)

    def fill(x_ref):
        h_ref[...] = (_ln(x_ref[...]) * (1.0 + sc_ref[0]) + sh_ref[0]).astype(BF16)

    @pl.when((pl.program_id(1) == 0) & (i < n_prompt_tiles))
    def _():
        fill(xp_ref)

    @pl.when((pl.program_id(1) == 0) & (i >= n_prompt_tiles))
    def _():
        fill(xs_ref)

    o_ref[...] = _dot(h_ref[...], w_ref[...]).astype(o_ref.dtype)


def _ln_win(xp, xs, mod3, w_in_r, tile_seq, tm, tn):
    d = xp.shape[1]
    t = xp.shape[0] + xs.shape[0]
    npt = xp.shape[0] // tm
    n = w_in_r.shape[1]
    pspec, sspec = _two_group_specs(tm, d, npt)
    grid_spec = pltpu.PrefetchScalarGridSpec(
        num_scalar_prefetch=1,
        grid=(t // tm, n // tn),
        in_specs=[
            pspec, sspec,
            pl.BlockSpec((1, 1, d), lambda i, j, s: (s[i] * 6 + 0, 0, 0)),
            pl.BlockSpec((1, 1, d), lambda i, j, s: (s[i] * 6 + 1, 0, 0)),
            pl.BlockSpec((d, tn), lambda i, j, s: (0, j)),
        ],
        out_specs=pl.BlockSpec((tm, tn), lambda i, j, s: (i, j)),
        scratch_shapes=[pltpu.VMEM((tm, d), BF16)],
    )
    return pl.pallas_call(
        functools.partial(_lnwin_kernel, n_prompt_tiles=npt),
        grid_spec=grid_spec,
        out_shape=jax.ShapeDtypeStruct((t, n), BF16),
        compiler_params=_params("arbitrary", "arbitrary"),
        name="ln_win",
    )(tile_seq, xp, xs, mod3, mod3, w_in_r)


def _rope(t, cosf, sinf):
    return t * cosf + pltpu.roll(t, HEAD_DIM // 2, 1) * sinf


def _attn_kernel(first_ref, last_ref, pblk_ref, sink_ref, q_ref, kp_ref, kc_ref, kn_ref,
                 vp_ref, vc_ref, vn_ref, cp_ref, cc_ref, cn_ref, sp_ref, sc_ref, sn_ref, o_ref):
    n = pl.program_id(0)
    blk = WINDOW_BLOCK
    is_first = first_ref[n] == 1
    is_last = last_ref[n] == 1
    col_lo = jnp.where(is_first, blk, 0)
    col_hi = jnp.where(is_last, 2 * blk, 3 * blk)
    rows = ATT_GROUP * blk
    r = lax.broadcasted_iota(jnp.int32, (rows, 3 * blk), 0) & (blk - 1)
    c = lax.broadcasted_iota(jnp.int32, (rows, 3 * blk), 1)
    d = c - r
    valid = (d >= 0) & (d <= 2 * blk) & (c >= col_lo) & (c < col_hi)
    grp = lax.broadcasted_iota(jnp.int32, (rows, 1), 0) // blk
    cosc, sinc = cc_ref[...], sc_ref[...]
    scale = HEAD_DIM ** -0.5
    for g in range(ATT_KV_HEADS):
        ks = slice(g * HEAD_DIM, (g + 1) * HEAD_DIM)
        kband = jnp.concatenate([
            _rope(kp_ref[:, ks].astype(F32), cp_ref[...], sp_ref[...]).astype(BF16),
            _rope(kc_ref[:, ks].astype(F32), cosc, sinc).astype(BF16),
            _rope(kn_ref[:, ks].astype(F32), cn_ref[...], sn_ref[...]).astype(BF16)], axis=0)
        vband = jnp.concatenate([vp_ref[:, ks], vc_ref[:, ks], vn_ref[:, ks]], axis=0)
        qs = []
        sink_col = jnp.zeros((rows, 1), F32)
        for j in range(ATT_GROUP):
            h = g * ATT_GROUP + j
            qh = q_ref[:, h * HEAD_DIM:(h + 1) * HEAD_DIM].astype(F32)
            qs.append((_rope(qh, cosc, sinc) * scale).astype(BF16))
            sink_col = jnp.where(grp == j, sink_ref[h], sink_col)
        qg = jnp.concatenate(qs, axis=0)
        s = _dot_nt(qg, kband)
        s = jnp.where(valid, s, NEG_BIG)
        m = jnp.maximum(jnp.max(s, axis=-1, keepdims=True), sink_col)
        p = jnp.exp(s - m)
        den = jnp.sum(p, axis=-1, keepdims=True) + jnp.exp(sink_col - m)
        o = _dot(p.astype(BF16), vband) / den
        for j in range(ATT_GROUP):
            h = g * ATT_GROUP + j
            o_ref[:, h * HEAD_DIM:(h + 1) * HEAD_DIM] = o[j * blk:(j + 1) * blk].astype(o_ref.dtype)


def _attention(z, sink, cosf, sinf, blk_first, blk_last, blk_pos):
    t = z.shape[0]
    blk = WINDOW_BLOCK
    nblk = t // blk
    npos = cosf.shape[0] // blk
    kcol = Z_AK // ATT_KV
    vcol = Z_AV // ATT_KV
    prev = lambda n: jnp.maximum(n - 1, 0)
    nxt = lambda n: jnp.minimum(n + 1, nblk - 1)
    pprev = lambda p: jnp.maximum(p - 1, 0)
    pnxt = lambda p: jnp.minimum(p + 1, npos - 1)
    tab = lambda f: pl.BlockSpec((blk, HEAD_DIM), lambda n, a, b, p: (f(p[n]), 0))
    same = lambda p: p
    grid_spec = pltpu.PrefetchScalarGridSpec(
        num_scalar_prefetch=3,
        grid=(nblk,),
        in_specs=[
            pl.BlockSpec(memory_space=pltpu.SMEM),
            pl.BlockSpec((blk, ATT_Q), lambda n, a, b, p: (n, Z_AQ // ATT_Q)),
            pl.BlockSpec((blk, ATT_KV), lambda n, a, b, p: (prev(n), kcol)),
            pl.BlockSpec((blk, ATT_KV), lambda n, a, b, p: (n, kcol)),
            pl.BlockSpec((blk, ATT_KV), lambda n, a, b, p: (nxt(n), kcol)),
            pl.BlockSpec((blk, ATT_KV), lambda n, a, b, p: (prev(n), vcol)),
            pl.BlockSpec((blk, ATT_KV), lambda n, a, b, p: (n, vcol)),
            pl.BlockSpec((blk, ATT_KV), lambda n, a, b, p: (nxt(n), vcol)),
            tab(pprev), tab(same), tab(pnxt),
            tab(pprev), tab(same), tab(pnxt),
        ],
        out_specs=pl.BlockSpec((blk, ATT_Q), lambda n, a, b, p: (n, 0)),
    )
    return pl.pallas_call(
        _attn_kernel,
        grid_spec=grid_spec,
        out_shape=jax.ShapeDtypeStruct((t, ATT_Q), BF16),
        compiler_params=_params("arbitrary"),
        name="win_attn",
    )(blk_first, blk_last, blk_pos, sink, z, z, z, z, z, z, z, cosf, cosf, cosf, sinf, sinf, sinf)


GLA_CHUNK = 128
GLA_SUB = 16
GLA_NSUB = GLA_CHUNK // GLA_SUB


def _logsig(z):
    return -(jnp.maximum(-z, 0.0) + jnp.log1p(jnp.exp(-jnp.abs(z))))


def _mask_sums(masks, g):
    g_hi = g.astype(BF16)
    g_lo = (g - g_hi.astype(F32)).astype(BF16)
    return _dot(masks, g_hi) + _dot(masks, g_lo)


def _gla_intra_kernel(q_ref, k_ref, v_ref, l_ref, w2f_ref, bf_ref, w2b_ref, bb_ref,
                      qtf_ref, kdf_ref, qtb_ref, kdb_ref, decf_ref, decb_ref, oi_ref):
    c, sub, nsub = GLA_CHUNK, GLA_SUB, GLA_NSUB
    one = lambda m: jnp.where(m, 1.0, 0.0).astype(BF16)
    ri = lax.broadcasted_iota(jnp.int32, (c, c), 0)
    ci = lax.broadcasted_iota(jnp.int32, (c, c), 1)
    blk0 = (ri // sub) * sub
    bi = lax.broadcasted_iota(jnp.int32, (PACK, c), 0) * sub
    cj = lax.broadcasted_iota(jnp.int32, (PACK, c), 1)
    masks_f = jnp.concatenate([one(ci <= ri), one((ci > blk0) & (ci <= ri)), one(cj <= bi)], axis=0)
    masks_b = jnp.concatenate(
        [one(ci >= ri), one((ci >= ri) & (ci < blk0 + sub - 1)), one(cj >= bi + sub - 1)], axis=0)

    glow = l_ref[...]
    gf = _logsig(_dot(glow[:, 0:GLA_GATE_RANK], w2f_ref[...]) + bf_ref[...]) * (1.0 / GLA_TAU)
    gb = _logsig(_dot(glow[:, GLA_GATE_RANK:2 * GLA_GATE_RANK], w2b_ref[...]) + bb_ref[...]) * (1.0 / GLA_TAU)
    sums_f = _mask_sums(masks_f, gf)
    sums_b = _mask_sums(masks_b, gb)
    q = q_ref[...].astype(F32) * (GLA_DK ** -0.5)
    k = k_ref[...].astype(F32)

    rr = lax.broadcasted_iota(jnp.int32, (sub, c), 0)
    cc = lax.broadcasted_iota(jnp.int32, (sub, c), 1)
    zero_blk = jnp.zeros((sub, GLA_DK), BF16)

    def direction(sums, reverse, qt_ref, kd_ref, dec_ref):
        b, win, ref = sums[0:c], sums[c:2 * c], sums[2 * c:2 * c + PACK]
        edge = b[0:1] if reverse else b[c - 1:c]
        qt_ref[...] = (q * jnp.exp(b)).astype(BF16)
        kd_ref[...] = (k * jnp.exp(edge - b)).astype(BF16)
        dec_ref[0] = jnp.exp(edge)
        qw = (q * jnp.exp(win)).astype(BF16)
        kw = k * jnp.exp(-win)
        att = [[] for _ in range(GLA_HEADS)]
        for i in range(nsub):
            dmat = jnp.exp(jnp.minimum(ref[i:i + 1] - ref, 0.0))
            live = range(i, nsub) if reverse else range(0, i + 1)
            for hd in range(GLA_HEADS):
                sl = slice(hd * GLA_DK, (hd + 1) * GLA_DK)
                kh = jnp.concatenate(
                    [(kw[j * sub:(j + 1) * sub, sl] * dmat[j:j + 1, sl]).astype(BF16) if j in live else zero_blk
                     for j in range(nsub)], axis=0)
                a = _dot_nt(qw[i * sub:(i + 1) * sub, sl], kh)
                keep = (cc >= rr + i * sub) if reverse else (cc <= rr + i * sub)
                att[hd].append(jnp.where(keep, a, 0.0))
        return [jnp.concatenate(rows, axis=0) for rows in att]

    att_f = direction(sums_f, False, qtf_ref, kdf_ref, decf_ref)
    att_b = direction(sums_b, True, qtb_ref, kdb_ref, decb_ref)
    for hd in range(GLA_HEADS):
        vsl = slice(hd * GLA_DV, (hd + 1) * GLA_DV)
        oi_ref[:, vsl] = _dot((att_f[hd] + att_b[hd]).astype(BF16), v_ref[:, vsl]).astype(oi_ref.dtype)


def _gla_intra(z, w2f, bf, w2b, bb):
    t = z.shape[0]
    c = GLA_CHUNK
    nch = t // c
    zspec = lambda w, off: pl.BlockSpec((c, w), lambda n: (n, off // w))
    wspec = pl.BlockSpec((GLA_GATE_RANK, GLA_K), lambda n: (0, 0))
    bspec = pl.BlockSpec((1, GLA_K), lambda n: (0, 0))
    tok = lambda w: pl.BlockSpec((c, w), lambda n: (n, 0))
    dec = pl.BlockSpec((1, 1, GLA_K), lambda n: (n, 0, 0))
    tk = jax.ShapeDtypeStruct((t, GLA_K), BF16)
    dk = jax.ShapeDtypeStruct((nch, 1, GLA_K), F32)
    return pl.pallas_call(
        _gla_intra_kernel,
        grid=(nch,),
        in_specs=[zspec(GLA_K, Z_GQ), zspec(GLA_K, Z_GK), zspec(GLA_V, Z_GV), zspec(LANES, Z_GLOW),
                  wspec, bspec, wspec, bspec],
        out_specs=[tok(GLA_K), tok(GLA_K), tok(GLA_K), tok(GLA_K), dec, dec, tok(GLA_V)],
        out_shape=[tk, tk, tk, tk, dk, dk, jax.ShapeDtypeStruct((t, GLA_V), BF16)],
        compiler_params=_params("arbitrary"),
        name="gla_intra",
    )(z, z, z, z, w2f, bf, w2b, bb)


def _gla_state_kernel(first_ref, last_ref, qf_ref, kf_ref, vf_ref, df_ref, qb_ref, kb_ref, vb_ref, db_ref, oi_ref,
                      of_ref, ob_ref, sf_ref, sb_ref):
    n = pl.program_id(0)
    nch = pl.num_programs(0)

    @pl.when(first_ref[n] == 1)
    def _():
        sf_ref[...] = jnp.zeros_like(sf_ref)

    @pl.when(last_ref[nch - 1 - n] == 1)
    def _():
        sb_ref[...] = jnp.zeros_like(sb_ref)

    for hd in range(GLA_HEADS):
        ksl = slice(hd * GLA_DK, (hd + 1) * GLA_DK)
        vsl = slice(hd * GLA_DV, (hd + 1) * GLA_DV)
        for q_ref, k_ref, v_ref, d_ref, o_ref, s_ref in (
                (qf_ref, kf_ref, vf_ref, df_ref, of_ref, sf_ref), (qb_ref, kb_ref, vb_ref, db_ref, ob_ref, sb_ref)):
            state = s_ref[hd]
            inter = _dot_nt(q_ref[:, ksl], state.astype(BF16))
            if o_ref is of_ref:
                inter = inter + oi_ref[:, vsl].astype(F32)
            o_ref[:, vsl] = inter.astype(o_ref.dtype)
            s_ref[hd] = d_ref[0][:, ksl] * state + _dot_tn(v_ref[:, vsl], k_ref[:, ksl])


def _gla_state(z, qtf, kdf, decf, qtb, kdb, decb, o_intra, ch_first, ch_last):
    t = z.shape[0]
    c = GLA_CHUNK
    nch = t // c
    fwd = lambda n: n
    bwd = lambda n: nch - 1 - n
    tok = lambda f: pl.BlockSpec((c, GLA_K), lambda n, a, b: (f(n), 0))
    val = lambda f: pl.BlockSpec((c, GLA_V), lambda n, a, b: (f(n), Z_GV // GLA_V))
    dec = lambda f: pl.BlockSpec((1, 1, GLA_K), lambda n, a, b: (f(n), 0, 0))
    out = lambda f: pl.BlockSpec((c, GLA_V), lambda n, a, b: (f(n), 0))
    grid_spec = pltpu.PrefetchScalarGridSpec(
        num_scalar_prefetch=2,
        grid=(nch,),
        in_specs=[tok(fwd), tok(fwd), val(fwd), dec(fwd), tok(bwd), tok(bwd), val(bwd), dec(bwd), out(fwd)],
        out_specs=[out(fwd), out(bwd)],
        scratch_shapes=[pltpu.VMEM((GLA_HEADS, GLA_DV, GLA_DK), F32), pltpu.VMEM((GLA_HEADS, GLA_DV, GLA_DK), F32)],
    )
    o = jax.ShapeDtypeStruct((t, GLA_V), BF16)
    return pl.pallas_call(
        _gla_state_kernel,
        grid_spec=grid_spec,
        out_shape=[o, o],
        compiler_params=_params("arbitrary"),
        name="gla_state",
    )(ch_first, ch_last, qtf, kdf, z, decf, qtb, kdb, z, decb, o_intra)


def _merge_kernel(seq_ref, oa_ref, of_ref, ob_ref, gr_ref, ga_ref, gb_ref, xp_ref, xs_ref, nw_ref,
                  wa_ref, wb_ref, wo_ref, g1_ref, l1g_ref, l1b_ref, sh2_ref, sc2_ref, x1_ref, h2_ref,
                  *, n_prompt_tiles):
    i = pl.program_id(0)
    og = of_ref[...].astype(F32) + ob_ref[...].astype(F32)
    parts = []
    for h in range(GLA_HEADS):
        th = og[:, h * GLA_DV:(h + 1) * GLA_DV]
        ms = jnp.mean(th * th, axis=-1, keepdims=True)
        parts.append(th * lax.rsqrt(ms + RMS_EPS))
    gr = gr_ref[...].astype(F32)
    ogn = jnp.concatenate(parts, axis=1) * nw_ref[...] * (gr * jax.nn.sigmoid(gr))
    a = _dot(oa_ref[...], wa_ref[...])
    b = _dot(ogn.astype(BF16), wb_ref[...])
    merged = jax.nn.sigmoid(ga_ref[...].astype(F32)) * a + jax.nn.sigmoid(gb_ref[...].astype(F32)) * b
    mix = g1_ref[0] * _dot(merged.astype(BF16), wo_ref[...])

    def finish(x_ref):
        x1 = _ln(DN_ALPHA * x_ref[...] + mix) * l1g_ref[...] + l1b_ref[...]
        x1_ref[...] = x1
        h2_ref[...] = (_ln(x1) * (1.0 + sc2_ref[0]) + sh2_ref[0]).astype(BF16)

    @pl.when(i < n_prompt_tiles)
    def _():
        finish(xp_ref)

    @pl.when(i >= n_prompt_tiles)
    def _():
        finish(xs_ref)


def _merge(o_attn, o_f, o_b, z, xp, xs, mod3, norm_w, wa, wb, wo, ln1_g, ln1_b, tile_seq, tm):
    d = xp.shape[1]
    t = xp.shape[0] + xs.shape[0]
    npt = xp.shape[0] // tm
    tok = lambda col: pl.BlockSpec((tm, d), lambda i, s: (i, col))
    const = lambda shape: pl.BlockSpec(shape, lambda i, s: (0,) * len(shape), pipeline_mode=pl.Buffered(1))
    modspec = lambda k: pl.BlockSpec((1, 1, d), lambda i, s: (s[i] * 6 + k, 0, 0))
    pspec, sspec = _two_group_specs(tm, d, npt)
    grid_spec = pltpu.PrefetchScalarGridSpec(
        num_scalar_prefetch=1,
        grid=(t // tm,),
        in_specs=[tok(0), tok(0), tok(0), tok(Z_GR // d), tok(Z_GA // d), tok(Z_GB // d), pspec, sspec,
                  const((1, d)), const((d, d)), const((d, d)), const((d, d)),
                  modspec(2), const((1, d)), const((1, d)), modspec(3), modspec(4)],
        out_specs=[tok(0), tok(0)],
    )
    return pl.pallas_call(
        functools.partial(_merge_kernel, n_prompt_tiles=npt),
        grid_spec=grid_spec,
        out_shape=[jax.ShapeDtypeStruct((t, d), F32), jax.ShapeDtypeStruct((t, d), BF16)],
        compiler_params=_params("arbitrary"),
        name="merge_ln1",
    )(tile_seq, o_attn, o_f, o_b, z, z, z, xp, xs, norm_w, wa, wb, wo, mod3, ln1_g, ln1_b, mod3, mod3)


SLAB = 8
RANK_NONE = 127.0


def _slabs(s):
    return [s[i * SLAB:(i + 1) * SLAB] for i in range(s.shape[0] // SLAB)]


def _extract_desc(slabs, count, want_rank=False):
    vals = []
    ranks = [jnp.full(sl.shape, RANK_NONE, F32) for sl in slabs] if want_rank else None
    for it in range(count):
        m8 = functools.reduce(jnp.maximum, slabs)
        m = jnp.max(m8, axis=0, keepdims=True)
        vals.append(m)
        hit = [sl == m for sl in slabs]
        if want_rank:
            ranks = [jnp.where(h, float(it), r) for h, r in zip(hit, ranks)]
        slabs = [jnp.where(h, -jnp.inf, sl) for h, sl in zip(hit, slabs)]
    return (vals, ranks) if want_rank else vals


def _stack_rows(vals, rows, lanes):
    ridx = lax.broadcasted_iota(jnp.int32, (rows, lanes), 0)
    acc = jnp.full((rows, lanes), -jnp.inf, F32)
    for i, v in enumerate(vals):
        acc = jnp.where(ridx == i, v, acc)
    return acc


def _router_kernel(h2_ref, wq_ref, sk_ref, rk_ref, e2_ref, cnt_ref, c_ref):
    tm = h2_ref.shape[0]
    kx = PEER_TOPK
    q = _dot(h2_ref[...], wq_ref[...]).astype(BF16)
    ridx = lax.broadcasted_iota(jnp.int32, (SLAB, tm), 0)
    for h in range(PEER_HEADS):
        s1 = _dot_nt(sk_ref[h, 0], q[:, (2 * h) * PEER_NKEYS:(2 * h + 1) * PEER_NKEYS])
        s2 = _dot_nt(sk_ref[h, 1], q[:, (2 * h + 1) * PEER_NKEYS:(2 * h + 2) * PEER_NKEYS])
        v1 = _extract_desc(_slabs(s1), kx)
        v2, rank2 = _extract_desc(_slabs(s2), kx, want_rank=True)
        st1 = _stack_rows(v1, 2 * SLAB, tm)
        st2 = _stack_rows(v2, 2 * SLAB, tm)
        cand = []
        for a in range(SLAB):
            nb = kx // (a + 1)
            for sl in range((nb + SLAB - 1) // SLAB):
                piece = st2[sl * SLAB:(sl + 1) * SLAB] + v1[a]
                if nb < (sl + 1) * SLAB:
                    piece = jnp.where(ridx < nb - sl * SLAB, piece, -jnp.inf)
                cand.append(piece)
        cand.append(st1[SLAB:2 * SLAB] + v2[0])
        top = _extract_desc(cand, kx)
        zsum = functools.reduce(lambda acc, v: acc + jnp.exp(v - top[0]), top, jnp.zeros_like(top[0]))
        tau = top[kx - 1]
        cnt = jnp.zeros_like(s1)
        for b in range(SLAB):
            cnt = jnp.where(s1 + v2[b] >= tau, float(b + 1), cnt)
        best = jnp.full_like(v1[0], float(SLAB))
        for b in range(SLAB, kx):
            best = jnp.where(v1[0] + v2[b] >= tau, float(b + 1), best)
        cnt = jnp.where((s1 == v1[0]) & (cnt == float(SLAB)), best, cnt)
        rk_ref[h] = jnp.concatenate(rank2, axis=0).astype(BF16)
        e2_ref[h] = jnp.exp(s2 - v2[0]).astype(BF16)
        cnt_ref[h] = cnt
        c_ref[h] = jnp.exp(s1 - v1[0]) / zsum


def _router(h2, wq, subkeys, tm):
    t, d = h2.shape
    nq = wq.shape[1]
    shape = (PEER_HEADS, PEER_NKEYS, t)
    ospec = pl.BlockSpec((PEER_HEADS, PEER_NKEYS, tm), lambda i: (0, 0, i))
    return pl.pallas_call(
        _router_kernel,
        grid=(t // tm,),
        in_specs=[
            pl.BlockSpec((tm, d), lambda i: (i, 0)),
            pl.BlockSpec((d, nq), lambda i: (0, 0), pipeline_mode=pl.Buffered(1)),
            pl.BlockSpec(subkeys.shape, lambda i: (0, 0, 0, 0), pipeline_mode=pl.Buffered(1)),
        ],
        out_specs=[ospec, ospec, ospec, ospec],
        out_shape=[jax.ShapeDtypeStruct(shape, BF16), jax.ShapeDtypeStruct(shape, BF16),
                   jax.ShapeDtypeStruct(shape, F32), jax.ShapeDtypeStruct(shape, F32)],
        compiler_params=_params("arbitrary"),
        name="peer_router",
    )(h2, wq, subkeys)


_SQRT_HALF = math.sqrt(0.5)


def _gelu(x):
    return 0.5 * x * (1.0 + lax.erf(x * _SQRT_HALF))


def _cast_t_kernel(x_ref, o_ref):
    o_ref[...] = x_ref[...].T.astype(o_ref.dtype)


def _cast_transpose(x, tn):
    n, d = x.shape
    return pl.pallas_call(
        _cast_t_kernel,
        grid=(n // tn,),
        in_specs=[pl.BlockSpec((tn, d), lambda j: (j, 0))],
        out_specs=pl.BlockSpec((d, tn), lambda j: (0, j)),
        out_shape=jax.ShapeDtypeStruct((d, n), BF16),
        compiler_params=_params("arbitrary"),
        name="cast_transpose",
    )(x)


def _peer_first_kernel(u_ref, h2_ref, o_ref):
    o_ref[...] = _dot_nt(u_ref[...], h2_ref[...])


def _peer_first(h2, u_bf, tm, tn):
    d = h2.shape[1]
    return pl.pallas_call(
        _peer_first_kernel,
        grid=(1,),
        in_specs=[pl.BlockSpec((tn, d), lambda i: (0, 0)), pl.BlockSpec((tm, d), lambda i: (0, 0))],
        out_specs=pl.BlockSpec((tn, tm), lambda i: (0, 0)),
        out_shape=jax.ShapeDtypeStruct((tn, tm), F32),
        compiler_params=_params("arbitrary"),
        name="peer_first",
    )(u_bf, h2)


def _peer_kernel(seq_ref, h2n_ref, un_ref, vt_ref, rk_ref, e2_ref, cnt_ref, c_ref, at_init_ref, x1_ref, g2_ref,
                 l2g_ref, l2b_ref, op_ref, os_ref, at0_ref, at1_ref, wt_ref, acc_ref, *, n_prompt_tiles):
    i = pl.program_id(0)
    j = pl.program_id(1)
    tm = h2n_ref.shape[0]
    ni = un_ref.shape[0] // PEER_NKEYS
    groups = PEER_NKEYS // PACK

    @pl.when((i == 0) & (j == 0))
    def _():
        at0_ref[...] = at_init_ref[...]

    @pl.when(j == 0)
    def _():
        acc_ref[...] = jnp.zeros_like(acc_ref)

    def step(at_cur, at_next):
        at_next[...] = _dot_nt(un_ref[...], h2n_ref[...])
        for ii in range(ni):
            rows = slice(ii * PEER_NKEYS, (ii + 1) * PEER_NKEYS)
            gate = None
            for h in range(PEER_HEADS):
                cnt = jnp.broadcast_to(cnt_ref[h, ii:ii + 1, :], (PACK, tm)).astype(BF16)
                coef = jnp.broadcast_to(c_ref[h, ii:ii + 1, :], (PACK, tm)).astype(BF16)
                sel = jnp.where(rk_ref[h] < cnt[None], e2_ref[h] * coef[None], jnp.zeros((), BF16))
                gate = sel if gate is None else gate + sel
            act = _gelu(at_cur[rows, :]).astype(BF16)
            wt_ref[rows, :] = gate.reshape(PEER_NKEYS, tm) * act
        acc_ref[...] += _dot(vt_ref[...], wt_ref[...])

    @pl.when(j % 2 == 0)
    def _():
        step(at0_ref, at1_ref)

    @pl.when(j % 2 == 1)
    def _():
        step(at1_ref, at0_ref)

    @pl.when(j == pl.num_programs(1) - 1)
    def _():
        ff = acc_ref[...].T
        y = _ln(DN_ALPHA * x1_ref[...] + g2_ref[0] * ff) * l2g_ref[...] + l2b_ref[...]

        @pl.when(i < n_prompt_tiles)
        def _():
            op_ref[...] = y

        @pl.when(i >= n_prompt_tiles)
        def _():
            os_ref[...] = y


def _peer(h2, u_bf, vt_bf, rk, e2, cnt, coef, x1, mod3, ln2_g, ln2_b, tile_seq, tm, tn, t_prompt):
    t, d = x1.shape
    ne = u_bf.shape[0]
    ni = tn // PEER_NKEYS
    nti, ntj = t // tm, ne // tn
    assert ntj % 2 == 0, "the two A^T buffers alternate by the parity of the expert-tile index"
    npt = t_prompt // tm
    groups = PEER_NKEYS // PACK
    rk4 = rk.reshape(PEER_HEADS, groups, PACK, t)
    e24 = e2.reshape(PEER_HEADS, groups, PACK, t)
    at_init = _peer_first(h2, u_bf, tm, tn)
    nxt_i = lambda i, j: jnp.minimum(jnp.where(j == ntj - 1, i + 1, i), nti - 1)
    nxt_j = lambda j: jnp.where(j == ntj - 1, 0, j + 1)
    once = pl.Buffered(1)
    const = lambda shape: pl.BlockSpec(shape, lambda i, j, s: (0,) * len(shape), pipeline_mode=once)
    grid_spec = pltpu.PrefetchScalarGridSpec(
        num_scalar_prefetch=1,
        grid=(nti, ntj),
        in_specs=[
            pl.BlockSpec((tm, d), lambda i, j, s: (nxt_i(i, j), 0), pipeline_mode=once),
            pl.BlockSpec((tn, d), lambda i, j, s: (nxt_j(j), 0)),
            pl.BlockSpec((d, tn), lambda i, j, s: (0, j)),
            pl.BlockSpec((PEER_HEADS, groups, PACK, tm), lambda i, j, s: (0, 0, 0, i), pipeline_mode=once),
            pl.BlockSpec((PEER_HEADS, groups, PACK, tm), lambda i, j, s: (0, 0, 0, i), pipeline_mode=once),
            pl.BlockSpec((PEER_HEADS, ni, tm), lambda i, j, s: (0, j, i)),
            pl.BlockSpec((PEER_HEADS, ni, tm), lambda i, j, s: (0, j, i)),
            const((tn, tm)),
            pl.BlockSpec((tm, d), lambda i, j, s: (i, 0), pipeline_mode=once),
            pl.BlockSpec((1, 1, d), lambda i, j, s: (s[i] * 6 + 5, 0, 0)),
            const((1, d)), const((1, d)),
        ],
        out_specs=[pl.BlockSpec((tm, d), lambda i, j, s: (jnp.minimum(i, npt - 1), 0)),
                   pl.BlockSpec((tm, d), lambda i, j, s: (jnp.maximum(i - npt, 0), 0))],
        scratch_shapes=[pltpu.VMEM((tn, tm), F32), pltpu.VMEM((tn, tm), F32), pltpu.VMEM((tn, tm), BF16),
                        pltpu.VMEM((d, tm), F32)],
    )
    return pl.pallas_call(
        functools.partial(_peer_kernel, n_prompt_tiles=npt),
        grid_spec=grid_spec,
        out_shape=[jax.ShapeDtypeStruct((t_prompt, d), F32), jax.ShapeDtypeStruct((t - t_prompt, d), F32)],
        compiler_params=_params("arbitrary", "arbitrary"),
        name="peer_dense",
    )(tile_seq, h2, u_bf, vt_bf, rk4, e24, cnt, coef, at_init, x1, mod3, ln2_g, ln2_b)


def _seq_tables(seq_lens, unit):
    sid, first, last, pos = [], [], [], []
    for s, n in enumerate(seq_lens):
        k = n // unit
        sid += [s] * k
        first += [1] + [0] * (k - 1)
        last += [0] * (k - 1) + [1]
        pos += list(range(k))
    mk = lambda v: jnp.asarray(np.asarray(v, np.int32))
    return mk(sid), mk(first), mk(last), mk(pos)


def _pick(n, options):
    for o in options:
        if n % o == 0:
            return o
    raise ValueError(f"no tile size in {options} divides {n}")


def _reorder_w_in(w_in):
    pts = np.cumsum(IN_WIDTHS)[:-1]
    aq, ak, av, gq, gk, gv, lf, lb, gr, ga, gb = jnp.split(w_in, [int(p) for p in pts], axis=1)
    pad = jnp.zeros((w_in.shape[0], Z_WIDTH - Z_GLOW - 2 * GLA_GATE_RANK), w_in.dtype)
    return jnp.concatenate([aq, gv, gr, ga, gb, gq, gk, ak, av, lf, lb, pad], axis=1).astype(BF16)


def _layer(xp, xs, c_pad, seq_lens, w_ada, b_ada, w_in, attn_sink, gla_w2_fwd, gla_b_fwd, gla_w2_bwd, gla_b_bwd,
           gla_norm_w, w_branch_attn, w_branch_gla, w_out, ln1_g, ln1_b, peer_w_query, peer_subkeys,
           peer_u, peer_v, ln2_g, ln2_b):
    d = xp.shape[1]
    gcd_len = functools.reduce(math.gcd, seq_lens)
    tm_in = _pick(gcd_len, (512, 256, 128))
    tm_merge = _pick(gcd_len, (256, 128))
    tm_router = _pick(gcd_len, (256, 128))
    tm_peer = _pick(gcd_len, (512, 256, 128))

    mod = _ada(c_pad, w_ada, b_ada)
    mod3 = mod.reshape(c_pad.shape[0] * 6, 1, d)

    seq_in = _seq_tables(seq_lens, tm_in)[0]
    z = _ln_win(xp, xs, mod3, _reorder_w_in(w_in), seq_in, tm_in, Z_WIDTH // 7)

    _, blk_first, blk_last, blk_pos = _seq_tables(seq_lens, WINDOW_BLOCK)
    half = HEAD_DIM // 2
    inv = ROPE_THETA ** (-jnp.arange(half, dtype=F32) / half)
    ang = jnp.arange(max(seq_lens), dtype=F32)[:, None] * inv[None, :]
    cosf = jnp.concatenate([jnp.cos(ang), jnp.cos(ang)], axis=1)
    sinf = jnp.concatenate([-jnp.sin(ang), jnp.sin(ang)], axis=1)
    o_attn = _attention(z, attn_sink.astype(F32), cosf, sinf, blk_first, blk_last, blk_pos)

    _, ch_first, ch_last, _ = _seq_tables(seq_lens, GLA_CHUNK)
    qtf, kdf, qtb, kdb, decf, decb, o_intra = _gla_intra(
        z, gla_w2_fwd.astype(BF16), gla_b_fwd.reshape(1, GLA_K), gla_w2_bwd.astype(BF16), gla_b_bwd.reshape(1, GLA_K))
    o_f, o_b = _gla_state(z, qtf, kdf, decf, qtb, kdb, decb, o_intra, ch_first, ch_last)

    seq_merge = _seq_tables(seq_lens, tm_merge)[0]
    row = lambda v: v.reshape(1, d)
    x1, h2 = _merge(o_attn, o_f, o_b, z, xp, xs, mod3, row(gla_norm_w), w_branch_attn.astype(BF16),
                    w_branch_gla.astype(BF16), w_out.astype(BF16), row(ln1_g), row(ln1_b), seq_merge, tm_merge)

    rk, e2, cnt, coef = _router(h2, peer_w_query.astype(BF16), peer_subkeys.astype(BF16), tm_router)

    seq_peer = _seq_tables(seq_lens, tm_peer)[0]
    return _peer(h2, peer_u.astype(BF16), _cast_transpose(peer_v, 512), rk, e2, cnt, coef, x1, mod3,
                 row(ln2_g), row(ln2_b), seq_peer, tm_peer, 1024, xp.shape[0])


def kernel(x_prompt, x_sample, c_prompt, c_sample, w_ada, b_ada, w_in, attn_sink, gla_w2_fwd, gla_b_fwd,
           gla_w2_bwd, gla_b_bwd, gla_norm_w, w_branch_attn, w_branch_gla, w_out, ln1_g, ln1_b, peer_w_query,
           peer_subkeys, peer_u, peer_v, ln2_g, ln2_b):
    assert w_ada.shape[0] == 1, "single-layer trunk"
    bp, sp, d = x_prompt.shape
    bs, ss, _ = x_sample.shape
    seq_lens = [sp] * bp + [ss] * bs
    nseq = bp + bs
    c_pad = jnp.zeros((-(-nseq // 8) * 8, d), F32).at[:nseq].set(jnp.concatenate([c_prompt, c_sample], axis=0))
    yp, ys = _layer(x_prompt.reshape(bp * sp, d), x_sample.reshape(bs * ss, d), c_pad, seq_lens, w_ada[0], b_ada[0],
                    w_in[0], attn_sink[0], gla_w2_fwd[0], gla_b_fwd[0], gla_w2_bwd[0], gla_b_bwd[0], gla_norm_w[0],
                    w_branch_attn[0], w_branch_gla[0], w_out[0], ln1_g[0], ln1_b[0], peer_w_query[0],
                    peer_subkeys[0], peer_u[0], peer_v[0], ln2_g[0], ln2_b[0])
    return (yp.reshape(bp, sp, d), ys.reshape(bs, ss, d))
```

```python
import functools
import math

import numpy as np
import jax
import jax.numpy as jnp
from jax import lax
from jax.experimental import pallas as pl
from jax.experimental.pallas import tpu as pltpu

F32 = jnp.float32
BF16 = jnp.bfloat16

D_MODEL = 2048
ATT_HEADS = 16
ATT_KV_HEADS = 4
ATT_GROUP = ATT_HEADS // ATT_KV_HEADS
HEAD_DIM = 128
WINDOW_BLOCK = 128
ROPE_THETA = 10000.0
GLA_HEADS = 4
GLA_DK = 256
GLA_DV = 512
GLA_GATE_RANK = 16
GLA_TAU = 16.0
PEER_HEADS = 8
PEER_NKEYS = 128
PEER_N = PEER_NKEYS * PEER_NKEYS
PEER_TOPK = 16
DN_ALPHA = 2.0 ** 0.25
LN_EPS = 1e-5
RMS_EPS = 1e-6

ATT_Q = ATT_HEADS * HEAD_DIM
ATT_KV = ATT_KV_HEADS * HEAD_DIM
GLA_K = GLA_HEADS * GLA_DK
GLA_V = GLA_HEADS * GLA_DV
IN_WIDTHS = (ATT_Q, ATT_KV, ATT_KV, GLA_K, GLA_K, GLA_V, GLA_GATE_RANK, GLA_GATE_RANK, GLA_V, D_MODEL, D_MODEL)

Z_AQ = 0
Z_GV = 2048
Z_GR = 4096
Z_GA = 6144
Z_GB = 8192
Z_GQ = 10240
Z_GK = 11264
Z_AK = 12288
Z_AV = 12800
Z_GLOW = 13312
Z_WIDTH = 13440

LANES = 128
PACK = 16
VMEM_LIMIT = 56 * 1024 * 1024

NEG_BIG = -1e30


def _ln(x):
    mu = jnp.mean(x, axis=-1, keepdims=True)
    xc = x - mu
    var = jnp.mean(xc * xc, axis=-1, keepdims=True)
    return xc * lax.rsqrt(var + LN_EPS)


def _dot_nt(a, b):
    return lax.dot_general(a, b, (((1,), (1,)), ((), ())), preferred_element_type=F32)


def _dot_tn(a, b):
    return lax.dot_general(a, b, (((0,), (0,)), ((), ())), preferred_element_type=F32)


def _dot(a, b):
    return jnp.dot(a, b, preferred_element_type=F32)


def _params(*sem, flags=None, vmem=VMEM_LIMIT):
    return pltpu.CompilerParams(dimension_semantics=sem, vmem_limit_bytes=vmem, flags=flags)


def _ada_kernel(c_ref, w_ref, b_ref, o_ref):
    c = c_ref[...]
    a = (c * jax.nn.sigmoid(c)).astype(BF16)
    o_ref[...] = _dot(a, w_ref[...].astype(BF16)) + b_ref[...]


def _ada(c_pad, w_ada, b_ada, tn=1024):
    rows, d = c_pad.shape
    n = w_ada.shape[1]
    return pl.pallas_call(
        _ada_kernel,
        grid=(n // tn,),
        in_specs=[
            pl.BlockSpec((rows, d), lambda j: (0, 0)),
            pl.BlockSpec((d, tn), lambda j: (0, j)),
            pl.BlockSpec((1, tn), lambda j: (0, j)),
        ],
        out_specs=pl.BlockSpec((rows, tn), lambda j: (0, j)),
        out_shape=jax.ShapeDtypeStruct((rows, n), F32),
        compiler_params=_params("arbitrary"),
        name="ada_mod",
    )(c_pad, w_ada, b_ada.reshape(1, n))


def _two_group_specs(tm, d, n_prompt_tiles, mode=None):
    pm = lambda i, *_: (jnp.minimum(i, n_prompt_tiles - 1), 0)
    sm = lambda i, *_: (jnp.maximum(i - n_prompt_tiles, 0), 0)
    return pl.BlockSpec((tm, d), pm, pipeline_mode=mode), pl.BlockSpec((tm, d), sm, pipeline_mode=mode)


def _lnwin_kernel(seq_ref, xp_ref, xs_ref, sh_ref, sc_ref, w_ref, o_ref, h_ref, *, n_prompt_tiles):
    i = pl.program_id(0)

    def fill(x_ref):
        h_ref[...] = (_ln(x_ref[...]) * (1.0 + sc_ref[0]) + sh_ref[0]).astype(BF16)

    @pl.when((pl.program_id(1) == 0) & (i < n_prompt_tiles))
    def _():
        fill(xp_ref)

    @pl.when((pl.program_id(1) == 0) & (i >= n_prompt_tiles))
    def _():
        fill(xs_ref)

    o_ref[...] = _dot(h_ref[...], w_ref[...]).astype(o_ref.dtype)


def _ln_win(xp, xs, mod3, w_in_r, tile_seq, tm, tn):
    d = xp.shape[1]
    t = xp.shape[0] + xs.shape[0]
    npt = xp.shape[0] // tm
    n = w_in_r.shape[1]
    pspec, sspec = _two_group_specs(tm, d, npt)
    grid_spec = pltpu.PrefetchScalarGridSpec(
        num_scalar_prefetch=1,
        grid=(t // tm, n // tn),
        in_specs=[
            pspec, sspec,
            pl.BlockSpec((1, 1, d), lambda i, j, s: (s[i] * 6 + 0, 0, 0)),
            pl.BlockSpec((1, 1, d), lambda i, j, s: (s[i] * 6 + 1, 0, 0)),
            pl.BlockSpec((d, tn), lambda i, j, s: (0, j)),
        ],
        out_specs=pl.BlockSpec((tm, tn), lambda i, j, s: (i, j)),
        scratch_shapes=[pltpu.VMEM((tm, d), BF16)],
    )
    return pl.pallas_call(
        functools.partial(_lnwin_kernel, n_prompt_tiles=npt),
        grid_spec=grid_spec,
        out_shape=jax.ShapeDtypeStruct((t, n), BF16),
        compiler_params=_params("arbitrary", "arbitrary"),
        name="ln_win",
    )(tile_seq, xp, xs, mod3, mod3, w_in_r)


def _rope(t, cosf, sinf):
    return t * cosf + pltpu.roll(t, HEAD_DIM // 2, 1) * sinf


def _attn_bias():
    blk = WINDOW_BLOCK
    r = (np.arange(ATT_GROUP * blk) % blk)[:, None]
    c = np.arange(3 * blk)[None, :]
    band = (c - r >= 0) & (c - r <= 2 * blk)
    out = []
    for var in range(4):
        lo = blk if var & 1 else 0
        hi = 2 * blk if var & 2 else 3 * blk
        out.append(np.where(band & (c >= lo) & (c < hi), 0.0, NEG_BIG))
    return jnp.asarray(np.stack(out).astype(np.float32))


def _attn_kernel(var_ref, pblk_ref, sink_ref, bias_ref, q_ref, kp_ref, kc_ref, kn_ref,
                 vp_ref, vc_ref, vn_ref, cp_ref, cc_ref, cn_ref, sp_ref, sc_ref, sn_ref, o_ref):
    blk = WINDOW_BLOCK
    rows = ATT_GROUP * blk
    bias = bias_ref[0]
    grp = lax.broadcasted_iota(jnp.int32, (rows, 1), 0) // blk
    cosc, sinc = cc_ref[...], sc_ref[...]
    scale = HEAD_DIM ** -0.5
    ones = jnp.ones((3 * blk, HEAD_DIM), BF16)
    for g in range(ATT_KV_HEADS):
        ks = slice(g * HEAD_DIM, (g + 1) * HEAD_DIM)
        kband = jnp.concatenate([
            _rope(kp_ref[:, ks].astype(F32), cp_ref[...], sp_ref[...]).astype(BF16),
            _rope(kc_ref[:, ks].astype(F32), cosc, sinc).astype(BF16),
            _rope(kn_ref[:, ks].astype(F32), cn_ref[...], sn_ref[...]).astype(BF16)], axis=0)
        vband = jnp.concatenate(
            [jnp.concatenate([vp_ref[:, ks], vc_ref[:, ks], vn_ref[:, ks]], axis=0), ones], axis=1)
        qs = []
        sink_col = jnp.zeros((rows, 1), F32)
        for j in range(ATT_GROUP):
            h = g * ATT_GROUP + j
            qh = q_ref[:, h * HEAD_DIM:(h + 1) * HEAD_DIM].astype(F32)
            qs.append((_rope(qh, cosc, sinc) * scale).astype(BF16))
            sink_col = jnp.where(grp == j, sink_ref[h], sink_col)
        qg = jnp.concatenate(qs, axis=0)
        s = _dot_nt(qg, kband) + bias
        m = jnp.maximum(jnp.max(s, axis=-1, keepdims=True), sink_col)
        p = jnp.exp(s - m)
        ov = _dot(p.astype(BF16), vband)
        den = ov[:, HEAD_DIM:HEAD_DIM + 1] + jnp.exp(sink_col - m)
        o = ov[:, 0:HEAD_DIM] / den
        for j in range(ATT_GROUP):
            h = g * ATT_GROUP + j
            o_ref[:, h * HEAD_DIM:(h + 1) * HEAD_DIM] = o[j * blk:(j + 1) * blk].astype(o_ref.dtype)


def _attention(z, sink, cosf, sinf, blk_var, blk_pos):
    t = z.shape[0]
    rows = ATT_GROUP * WINDOW_BLOCK
    blk = WINDOW_BLOCK
    nblk = t // blk
    npos = cosf.shape[0] // blk
    kcol = Z_AK // ATT_KV
    vcol = Z_AV // ATT_KV
    prev = lambda n: jnp.maximum(n - 1, 0)
    nxt = lambda n: jnp.minimum(n + 1, nblk - 1)
    pprev = lambda p: jnp.maximum(p - 1, 0)
    pnxt = lambda p: jnp.minimum(p + 1, npos - 1)
    tab = lambda f: pl.BlockSpec((blk, HEAD_DIM), lambda n, v, p: (f(p[n]), 0))
    same = lambda p: p
    grid_spec = pltpu.PrefetchScalarGridSpec(
        num_scalar_prefetch=2,
        grid=(nblk,),
        in_specs=[
            pl.BlockSpec(memory_space=pltpu.SMEM),
            pl.BlockSpec((1, rows, 3 * blk), lambda n, v, p: (v[n], 0, 0)),
            pl.BlockSpec((blk, ATT_Q), lambda n, v, p: (n, Z_AQ // ATT_Q)),
            pl.BlockSpec((blk, ATT_KV), lambda n, v, p: (prev(n), kcol)),
            pl.BlockSpec((blk, ATT_KV), lambda n, v, p: (n, kcol)),
            pl.BlockSpec((blk, ATT_KV), lambda n, v, p: (nxt(n), kcol)),
            pl.BlockSpec((blk, ATT_KV), lambda n, v, p: (prev(n), vcol)),
            pl.BlockSpec((blk, ATT_KV), lambda n, v, p: (n, vcol)),
            pl.BlockSpec((blk, ATT_KV), lambda n, v, p: (nxt(n), vcol)),
            tab(pprev), tab(same), tab(pnxt),
            tab(pprev), tab(same), tab(pnxt),
        ],
        out_specs=pl.BlockSpec((blk, ATT_Q), lambda n, v, p: (n, 0)),
    )
    return pl.pallas_call(
        _attn_kernel,
        grid_spec=grid_spec,
        out_shape=jax.ShapeDtypeStruct((t, ATT_Q), BF16),
        compiler_params=_params("arbitrary"),
        name="win_attn",
    )(blk_var, blk_pos, sink, _attn_bias(), z, z, z, z, z, z, z, cosf, cosf, cosf, sinf, sinf, sinf)


GLA_CHUNK = 128
GLA_SUB = 16
GLA_NSUB = GLA_CHUNK // GLA_SUB


def _logsig(z):
    return -(jnp.maximum(-z, 0.0) + jnp.log1p(jnp.exp(-jnp.abs(z))))


def _mask_sums(masks, g):
    g_hi = g.astype(BF16)
    g_lo = (g - g_hi.astype(F32)).astype(BF16)
    return _dot(masks, g_hi) + _dot(masks, g_lo)


def _gla_intra_kernel(q_ref, k_ref, v_ref, l_ref, w2f_ref, bf_ref, w2b_ref, bb_ref,
                      qtf_ref, kdf_ref, qtb_ref, kdb_ref, decf_ref, decb_ref, oi_ref):
    c, sub, nsub = GLA_CHUNK, GLA_SUB, GLA_NSUB
    one = lambda m: jnp.where(m, 1.0, 0.0).astype(BF16)
    ri = lax.broadcasted_iota(jnp.int32, (c, c), 0)
    ci = lax.broadcasted_iota(jnp.int32, (c, c), 1)
    blk0 = (ri // sub) * sub
    bi = lax.broadcasted_iota(jnp.int32, (PACK, c), 0) * sub
    cj = lax.broadcasted_iota(jnp.int32, (PACK, c), 1)
    masks_f = jnp.concatenate([one(ci <= ri), one((ci > blk0) & (ci <= ri)), one(cj <= bi)], axis=0)
    masks_b = jnp.concatenate(
        [one(ci >= ri), one((ci >= ri) & (ci < blk0 + sub - 1)), one(cj >= bi + sub - 1)], axis=0)

    glow = l_ref[...]
    gf = _logsig(_dot(glow[:, 0:GLA_GATE_RANK], w2f_ref[...]) + bf_ref[...]) * (1.0 / GLA_TAU)
    gb = _logsig(_dot(glow[:, GLA_GATE_RANK:2 * GLA_GATE_RANK], w2b_ref[...]) + bb_ref[...]) * (1.0 / GLA_TAU)
    sums_f = _mask_sums(masks_f, gf)
    sums_b = _mask_sums(masks_b, gb)
    q = q_ref[...].astype(F32) * (GLA_DK ** -0.5)
    k = k_ref[...].astype(F32)

    rr = lax.broadcasted_iota(jnp.int32, (sub, c), 0)
    cc = lax.broadcasted_iota(jnp.int32, (sub, c), 1)
    zero_blk = jnp.zeros((sub, GLA_DK), BF16)

    def direction(sums, reverse, qt_ref, kd_ref, dec_ref):
        b, win, ref = sums[0:c], sums[c:2 * c], sums[2 * c:2 * c + PACK]
        edge = b[0:1] if reverse else b[c - 1:c]
        qt_ref[...] = (q * jnp.exp(b)).astype(BF16)
        kd_ref[...] = (k * jnp.exp(edge - b)).astype(BF16)
        dec_ref[0] = jnp.exp(edge)
        qw = (q * jnp.exp(win)).astype(BF16)
        kw = k * jnp.exp(-win)
        att = [[] for _ in range(GLA_HEADS)]
        for i in range(nsub):
            dmat = jnp.exp(jnp.minimum(ref[i:i + 1] - ref, 0.0))
            live = range(i, nsub) if reverse else range(0, i + 1)
            for hd in range(GLA_HEADS):
                sl = slice(hd * GLA_DK, (hd + 1) * GLA_DK)
                kh = jnp.concatenate(
                    [(kw[j * sub:(j + 1) * sub, sl] * dmat[j:j + 1, sl]).astype(BF16) if j in live else zero_blk
                     for j in range(nsub)], axis=0)
                a = _dot_nt(qw[i * sub:(i + 1) * sub, sl], kh)
                keep = (cc >= rr + i * sub) if reverse else (cc <= rr + i * sub)
                att[hd].append(jnp.where(keep, a, 0.0))
        return [jnp.concatenate(rows, axis=0) for rows in att]

    att_f = direction(sums_f, False, qtf_ref, kdf_ref, decf_ref)
    att_b = direction(sums_b, True, qtb_ref, kdb_ref, decb_ref)
    for hd in range(GLA_HEADS):
        vsl = slice(hd * GLA_DV, (hd + 1) * GLA_DV)
        oi_ref[:, vsl] = _dot((att_f[hd] + att_b[hd]).astype(BF16), v_ref[:, vsl]).astype(oi_ref.dtype)


def _gla_intra(z, w2f, bf, w2b, bb):
    t = z.shape[0]
    c = GLA_CHUNK
    nch = t // c
    zspec = lambda w, off: pl.BlockSpec((c, w), lambda n: (n, off // w))
    wspec = pl.BlockSpec((GLA_GATE_RANK, GLA_K), lambda n: (0, 0))
    bspec = pl.BlockSpec((1, GLA_K), lambda n: (0, 0))
    tok = lambda w: pl.BlockSpec((c, w), lambda n: (n, 0))
    dec = pl.BlockSpec((1, 1, GLA_K), lambda n: (n, 0, 0))
    tk = jax.ShapeDtypeStruct((t, GLA_K), BF16)
    dk = jax.ShapeDtypeStruct((nch, 1, GLA_K), F32)
    return pl.pallas_call(
        _gla_intra_kernel,
        grid=(nch,),
        in_specs=[zspec(GLA_K, Z_GQ), zspec(GLA_K, Z_GK), zspec(GLA_V, Z_GV), zspec(LANES, Z_GLOW),
                  wspec, bspec, wspec, bspec],
        out_specs=[tok(GLA_K), tok(GLA_K), tok(GLA_K), tok(GLA_K), dec, dec, tok(GLA_V)],
        out_shape=[tk, tk, tk, tk, dk, dk, jax.ShapeDtypeStruct((t, GLA_V), BF16)],
        compiler_params=_params("arbitrary"),
        name="gla_intra",
    )(z, z, z, z, w2f, bf, w2b, bb)


def _gla_state_kernel(first_ref, last_ref, qf_ref, kf_ref, vf_ref, df_ref, qb_ref, kb_ref, vb_ref, db_ref, oi_ref,
                      of_ref, ob_ref, sf_ref, sb_ref):
    n = pl.program_id(0)
    nch = pl.num_programs(0)

    @pl.when(first_ref[n] == 1)
    def _():
        sf_ref[...] = jnp.zeros_like(sf_ref)

    @pl.when(last_ref[nch - 1 - n] == 1)
    def _():
        sb_ref[...] = jnp.zeros_like(sb_ref)

    for hd in range(GLA_HEADS):
        ksl = slice(hd * GLA_DK, (hd + 1) * GLA_DK)
        vsl = slice(hd * GLA_DV, (hd + 1) * GLA_DV)
        for q_ref, k_ref, v_ref, d_ref, o_ref, s_ref in (
                (qf_ref, kf_ref, vf_ref, df_ref, of_ref, sf_ref), (qb_ref, kb_ref, vb_ref, db_ref, ob_ref, sb_ref)):
            state = s_ref[hd]
            inter = _dot_nt(q_ref[:, ksl], state.astype(BF16))
            if o_ref is of_ref:
                inter = inter + oi_ref[:, vsl].astype(F32)
            o_ref[:, vsl] = inter.astype(o_ref.dtype)
            s_ref[hd] = d_ref[0][:, ksl] * state + _dot_tn(v_ref[:, vsl], k_ref[:, ksl])


def _gla_state(z, qtf, kdf, decf, qtb, kdb, decb, o_intra, ch_first, ch_last):
    t = z.shape[0]
    c = GLA_CHUNK
    nch = t // c
    fwd = lambda n: n
    bwd = lambda n: nch - 1 - n
    tok = lambda f: pl.BlockSpec((c, GLA_K), lambda n, a, b: (f(n), 0))
    val = lambda f: pl.BlockSpec((c, GLA_V), lambda n, a, b: (f(n), Z_GV // GLA_V))
    dec = lambda f: pl.BlockSpec((1, 1, GLA_K), lambda n, a, b: (f(n), 0, 0))
    out = lambda f: pl.BlockSpec((c, GLA_V), lambda n, a, b: (f(n), 0))
    grid_spec = pltpu.PrefetchScalarGridSpec(
        num_scalar_prefetch=2,
        grid=(nch,),
        in_specs=[tok(fwd), tok(fwd), val(fwd), dec(fwd), tok(bwd), tok(bwd), val(bwd), dec(bwd), out(fwd)],
        out_specs=[out(fwd), out(bwd)],
        scratch_shapes=[pltpu.VMEM((GLA_HEADS, GLA_DV, GLA_DK), F32), pltpu.VMEM((GLA_HEADS, GLA_DV, GLA_DK), F32)],
    )
    o = jax.ShapeDtypeStruct((t, GLA_V), BF16)
    return pl.pallas_call(
        _gla_state_kernel,
        grid_spec=grid_spec,
        out_shape=[o, o],
        compiler_params=_params("arbitrary"),
        name="gla_state",
    )(ch_first, ch_last, qtf, kdf, z, decf, qtb, kdb, z, decb, o_intra)


def _merge_kernel(seq_ref, oa_ref, of_ref, ob_ref, gr_ref, ga_ref, gb_ref, xp_ref, xs_ref, nw_ref,
                  wa_ref, wb_ref, wo_ref, g1_ref, l1g_ref, l1b_ref, sh2_ref, sc2_ref, x1_ref, h2_ref,
                  *, n_prompt_tiles):
    i = pl.program_id(0)
    og = of_ref[...].astype(F32) + ob_ref[...].astype(F32)
    parts = []
    for h in range(GLA_HEADS):
        th = og[:, h * GLA_DV:(h + 1) * GLA_DV]
        ms = jnp.mean(th * th, axis=-1, keepdims=True)
        parts.append(th * lax.rsqrt(ms + RMS_EPS))
    gr = gr_ref[...].astype(F32)
    ogn = jnp.concatenate(parts, axis=1) * nw_ref[...] * (gr * jax.nn.sigmoid(gr))
    a = _dot(oa_ref[...], wa_ref[...])
    b = _dot(ogn.astype(BF16), wb_ref[...])
    merged = jax.nn.sigmoid(ga_ref[...].astype(F32)) * a + jax.nn.sigmoid(gb_ref[...].astype(F32)) * b
    mix = g1_ref[0] * _dot(merged.astype(BF16), wo_ref[...])

    def finish(x_ref):
        x1 = _ln(DN_ALPHA * x_ref[...] + mix) * l1g_ref[...] + l1b_ref[...]
        x1_ref[...] = x1
        h2_ref[...] = (_ln(x1) * (1.0 + sc2_ref[0]) + sh2_ref[0]).astype(BF16)

    @pl.when(i < n_prompt_tiles)
    def _():
        finish(xp_ref)

    @pl.when(i >= n_prompt_tiles)
    def _():
        finish(xs_ref)


def _merge(o_attn, o_f, o_b, z, xp, xs, mod3, norm_w, wa, wb, wo, ln1_g, ln1_b, tile_seq, tm):
    d = xp.shape[1]
    t = xp.shape[0] + xs.shape[0]
    npt = xp.shape[0] // tm
    tok = lambda col: pl.BlockSpec((tm, d), lambda i, s: (i, col))
    const = lambda shape: pl.BlockSpec(shape, lambda i, s: (0,) * len(shape), pipeline_mode=pl.Buffered(1))
    modspec = lambda k: pl.BlockSpec((1, 1, d), lambda i, s: (s[i] * 6 + k, 0, 0))
    pspec, sspec = _two_group_specs(tm, d, npt)
    grid_spec = pltpu.PrefetchScalarGridSpec(
        num_scalar_prefetch=1,
        grid=(t // tm,),
        in_specs=[tok(0), tok(0), tok(0), tok(Z_GR // d), tok(Z_GA // d), tok(Z_GB // d), pspec, sspec,
                  const((1, d)), const((d, d)), const((d, d)), const((d, d)),
                  modspec(2), const((1, d)), const((1, d)), modspec(3), modspec(4)],
        out_specs=[tok(0), tok(0)],
    )
    return pl.pallas_call(
        functools.partial(_merge_kernel, n_prompt_tiles=npt),
        grid_spec=grid_spec,
        out_shape=[jax.ShapeDtypeStruct((t, d), F32), jax.ShapeDtypeStruct((t, d), BF16)],
        compiler_params=_params("arbitrary"),
        name="merge_ln1",
    )(tile_seq, o_attn, o_f, o_b, z, z, z, xp, xs, norm_w, wa, wb, wo, mod3, ln1_g, ln1_b, mod3, mod3)


SLAB = 8
RANK_NONE = 127.0


def _slabs(s):
    return [s[i * SLAB:(i + 1) * SLAB] for i in range(s.shape[0] // SLAB)]


def _extract_desc(slabs, count):
    vals = []
    for _ in range(count):
        m8 = functools.reduce(jnp.maximum, slabs)
        m = jnp.max(m8, axis=0, keepdims=True)
        vals.append(m)
        slabs = [jnp.where(sl == m, -jnp.inf, sl) for sl in slabs]
    return vals


def _sorting_network(n):
    pairs = []
    p = 1
    while p < n:
        k = p
        while k >= 1:
            for j in range(k % p, n - k, 2 * k):
                for i in range(min(k, n - j - k)):
                    if (i + j) // (2 * p) == (i + j + k) // (2 * p):
                        pairs.append((i + j, i + j + k))
            k //= 2
        p *= 2
    return pairs


def _extract_sorted(slabs, count):
    cols = list(slabs)
    for i, j in _sorting_network(len(cols)):
        cols[i], cols[j] = jnp.maximum(cols[i], cols[j]), jnp.minimum(cols[i], cols[j])
    vals = []
    for it in range(count):
        m = jnp.max(cols[0], axis=0, keepdims=True)
        vals.append(m)
        hit = cols[0] == m
        live = count - 1 - it
        cols = [jnp.where(hit, cols[k + 1], cols[k]) for k in range(live)]
    return vals


def _stack_rows(vals, rows, lanes):
    ridx = lax.broadcasted_iota(jnp.int32, (rows, lanes), 0)
    acc = jnp.full((rows, lanes), -jnp.inf, F32)
    for i, v in enumerate(vals):
        acc = jnp.where(ridx == i, v, acc)
    return acc


def _router_kernel(h2_ref, wq_ref, sk_ref, rk_ref, e2_ref, cnt_ref, c_ref):
    tm = h2_ref.shape[0]
    kx = PEER_TOPK
    q = _dot(h2_ref[...], wq_ref[...]).astype(BF16)
    ridx = lax.broadcasted_iota(jnp.int32, (SLAB, tm), 0)
    for h in range(PEER_HEADS):
        s1 = _dot_nt(sk_ref[h, 0], q[:, (2 * h) * PEER_NKEYS:(2 * h + 1) * PEER_NKEYS])
        s2 = _dot_nt(sk_ref[h, 1], q[:, (2 * h + 1) * PEER_NKEYS:(2 * h + 2) * PEER_NKEYS])
        v1 = _extract_sorted(_slabs(s1), kx)
        v2 = _extract_sorted(_slabs(s2), kx)
        rank2 = jnp.full_like(s2, RANK_NONE)
        for b in reversed(range(kx)):
            rank2 = jnp.where(s2 >= v2[b], float(b), rank2)
        st1 = _stack_rows(v1, 2 * SLAB, tm)
        st2 = _stack_rows(v2, 2 * SLAB, tm)
        cand = []
        for a in range(SLAB):
            nb = kx // (a + 1)
            for sl in range((nb + SLAB - 1) // SLAB):
                piece = st2[sl * SLAB:(sl + 1) * SLAB] + v1[a]
                if nb < (sl + 1) * SLAB:
                    piece = jnp.where(ridx < nb - sl * SLAB, piece, -jnp.inf)
                cand.append(piece)
        cand.append(st1[SLAB:2 * SLAB] + v2[0])
        top = _extract_desc(cand, kx)
        zsum = functools.reduce(lambda acc, v: acc + jnp.exp(v - top[0]), top, jnp.zeros_like(top[0]))
        tau = top[kx - 1]
        cnt = jnp.zeros_like(s1)
        for b in range(SLAB):
            cnt = jnp.where(s1 + v2[b] >= tau, float(b + 1), cnt)
        best = jnp.full_like(v1[0], float(SLAB))
        for b in range(SLAB, kx):
            best = jnp.where(v1[0] + v2[b] >= tau, float(b + 1), best)
        cnt = jnp.where((s1 == v1[0]) & (cnt == float(SLAB)), best, cnt)
        packed = (PEER_NKEYS // PACK, PACK, tm)
        rk_ref[h] = rank2.astype(BF16).reshape(packed)
        e2_ref[h] = jnp.exp(s2 - v2[0]).astype(BF16).reshape(packed)
        cnt_ref[h] = cnt
        c_ref[h] = jnp.exp(s1 - v1[0]) / zsum


def _router(h2, wq, subkeys, tm):
    t, d = h2.shape
    nq = wq.shape[1]
    shape = (PEER_HEADS, PEER_NKEYS, t)
    ospec = pl.BlockSpec((PEER_HEADS, PEER_NKEYS, tm), lambda i: (0, 0, i))
    pshape = (PEER_HEADS, PEER_NKEYS // PACK, PACK, t)
    pspec = pl.BlockSpec((PEER_HEADS, PEER_NKEYS // PACK, PACK, tm), lambda i: (0, 0, 0, i))
    return pl.pallas_call(
        _router_kernel,
        grid=(t // tm,),
        in_specs=[
            pl.BlockSpec((tm, d), lambda i: (i, 0)),
            pl.BlockSpec((d, nq), lambda i: (0, 0), pipeline_mode=pl.Buffered(1)),
            pl.BlockSpec(subkeys.shape, lambda i: (0, 0, 0, 0), pipeline_mode=pl.Buffered(1)),
        ],
        out_specs=[pspec, pspec, ospec, ospec],
        out_shape=[jax.ShapeDtypeStruct(pshape, BF16), jax.ShapeDtypeStruct(pshape, BF16),
                   jax.ShapeDtypeStruct(shape, F32), jax.ShapeDtypeStruct(shape, F32)],
        compiler_params=_params("arbitrary"),
        name="peer_router",
    )(h2, wq, subkeys)


_SQRT_HALF = math.sqrt(0.5)


def _gelu(x):
    return 0.5 * x * (1.0 + lax.erf(x * _SQRT_HALF))


def _cast_t_kernel(x_ref, o_ref):
    o_ref[...] = x_ref[...].T.astype(o_ref.dtype)


def _cast_transpose(x, tn):
    n, d = x.shape
    return pl.pallas_call(
        _cast_t_kernel,
        grid=(n // tn,),
        in_specs=[pl.BlockSpec((tn, d), lambda j: (j, 0))],
        out_specs=pl.BlockSpec((None, d, tn), lambda j: (j, 0, 0)),
        out_shape=jax.ShapeDtypeStruct((n // tn, d, tn), BF16),
        compiler_params=_params("arbitrary"),
        name="cast_transpose",
    )(x)


def _peer_first_kernel(u_ref, h2_ref, o_ref):
    o_ref[...] = _dot_nt(u_ref[...], h2_ref[...])


def _peer_first(h2, u_bf, tm, tn):
    d = h2.shape[1]
    return pl.pallas_call(
        _peer_first_kernel,
        grid=(1,),
        in_specs=[pl.BlockSpec((tn, d), lambda i: (0, 0)), pl.BlockSpec((tm, d), lambda i: (0, 0))],
        out_specs=pl.BlockSpec((tn, tm), lambda i: (0, 0)),
        out_shape=jax.ShapeDtypeStruct((tn, tm), F32),
        compiler_params=_params("arbitrary"),
        name="peer_first",
    )(u_bf, h2)


def _peer_kernel(seq_ref, h2n_ref, un_ref, vt_ref, rk_ref, e2_ref, cnt_ref, c_ref, at_init_ref, x1_ref, g2_ref,
                 l2g_ref, l2b_ref, op_ref, os_ref, at0_ref, at1_ref, wt_ref, acc_ref, *, n_prompt_tiles):
    i = pl.program_id(0)
    j = pl.program_id(1)
    tm = h2n_ref.shape[0]
    ni = un_ref.shape[0] // PEER_NKEYS
    groups = PEER_NKEYS // PACK

    @pl.when((i == 0) & (j == 0))
    def _():
        at0_ref[...] = at_init_ref[...]

    @pl.when(j == 0)
    def _():
        acc_ref[...] = jnp.zeros_like(acc_ref)

    def step(at_cur, at_next):
        at_next[...] = _dot_nt(un_ref[...], h2n_ref[...])
        for ii in range(ni):
            rows = slice(ii * PEER_NKEYS, (ii + 1) * PEER_NKEYS)
            gate = None
            for h in range(PEER_HEADS):
                cnt = jnp.broadcast_to(cnt_ref[h, ii:ii + 1, :], (PACK, tm)).astype(BF16)
                coef = jnp.broadcast_to(c_ref[h, ii:ii + 1, :], (PACK, tm)).astype(BF16)
                sel = jnp.where(rk_ref[h] < cnt[None], e2_ref[h] * coef[None], jnp.zeros((), BF16))
                gate = sel if gate is None else gate + sel
            act = _gelu(at_cur[rows, :]).astype(BF16)
            wt_ref[rows, :] = gate.reshape(PEER_NKEYS, tm) * act
        acc_ref[...] += _dot(vt_ref[...], wt_ref[...])

    @pl.when(j % 2 == 0)
    def _():
        step(at0_ref, at1_ref)

    @pl.when(j % 2 == 1)
    def _():
        step(at1_ref, at0_ref)

    @pl.when(j == pl.num_programs(1) - 1)
    def _():
        ff = acc_ref[...].T
        y = _ln(DN_ALPHA * x1_ref[...] + g2_ref[0] * ff) * l2g_ref[...] + l2b_ref[...]

        @pl.when(i < n_prompt_tiles)
        def _():
            op_ref[...] = y

        @pl.when(i >= n_prompt_tiles)
        def _():
            os_ref[...] = y


def _peer(h2, u_bf, vt_bf, rk, e2, cnt, coef, x1, mod3, ln2_g, ln2_b, tile_seq, tm, tn, t_prompt):
    t, d = x1.shape
    ne = u_bf.shape[0]
    ni = tn // PEER_NKEYS
    nti, ntj = t // tm, ne // tn
    assert ntj % 2 == 0, "the two A^T buffers alternate by the parity of the expert-tile index"
    npt = t_prompt // tm
    groups = PEER_NKEYS // PACK
    at_init = _peer_first(h2, u_bf, tm, tn)
    nxt_i = lambda i, j: jnp.minimum(jnp.where(j == ntj - 1, i + 1, i), nti - 1)
    nxt_j = lambda j: jnp.where(j == ntj - 1, 0, j + 1)
    once = pl.Buffered(1)
    const = lambda shape: pl.BlockSpec(shape, lambda i, j, s: (0,) * len(shape), pipeline_mode=once)
    grid_spec = pltpu.PrefetchScalarGridSpec(
        num_scalar_prefetch=1,
        grid=(nti, ntj),
        in_specs=[
            pl.BlockSpec((tm, d), lambda i, j, s: (nxt_i(i, j), 0), pipeline_mode=once),
            pl.BlockSpec((tn, d), lambda i, j, s: (nxt_j(j), 0)),
            pl.BlockSpec((None, d, tn), lambda i, j, s: (j, 0, 0)),
            pl.BlockSpec((PEER_HEADS, groups, PACK, tm), lambda i, j, s: (0, 0, 0, i), pipeline_mode=once),
            pl.BlockSpec((PEER_HEADS, groups, PACK, tm), lambda i, j, s: (0, 0, 0, i), pipeline_mode=once),
            pl.BlockSpec((PEER_HEADS, ni, tm), lambda i, j, s: (0, j, i)),
            pl.BlockSpec((PEER_HEADS, ni, tm), lambda i, j, s: (0, j, i)),
            const((tn, tm)),
            pl.BlockSpec((tm, d), lambda i, j, s: (i, 0), pipeline_mode=once),
            pl.BlockSpec((1, 1, d), lambda i, j, s: (s[i] * 6 + 5, 0, 0)),
            const((1, d)), const((1, d)),
        ],
        out_specs=[pl.BlockSpec((tm, d), lambda i, j, s: (jnp.minimum(i, npt - 1), 0)),
                   pl.BlockSpec((tm, d), lambda i, j, s: (jnp.maximum(i - npt, 0), 0))],
        scratch_shapes=[pltpu.VMEM((tn, tm), F32), pltpu.VMEM((tn, tm), F32), pltpu.VMEM((tn, tm), BF16),
                        pltpu.VMEM((d, tm), F32)],
    )
    return pl.pallas_call(
        functools.partial(_peer_kernel, n_prompt_tiles=npt),
        grid_spec=grid_spec,
        out_shape=[jax.ShapeDtypeStruct((t_prompt, d), F32), jax.ShapeDtypeStruct((t - t_prompt, d), F32)],
        compiler_params=_params("arbitrary", "arbitrary"),
        name="peer_dense",
    )(tile_seq, h2, u_bf, vt_bf, rk, e2, cnt, coef, at_init, x1, mod3, ln2_g, ln2_b)


def _seq_tables(seq_lens, unit):
    sid, first, last, pos = [], [], [], []
    for s, n in enumerate(seq_lens):
        k = n // unit
        sid += [s] * k
        first += [1] + [0] * (k - 1)
        last += [0] * (k - 1) + [1]
        pos += list(range(k))
    mk = lambda v: jnp.asarray(np.asarray(v, np.int32))
    return mk(sid), mk(first), mk(last), mk(pos)


def _pick(n, options):
    for o in options:
        if n % o == 0:
            return o
    raise ValueError(f"no tile size in {options} divides {n}")


def _reorder_w_in(w_in):
    pts = np.cumsum(IN_WIDTHS)[:-1]
    aq, ak, av, gq, gk, gv, lf, lb, gr, ga, gb = jnp.split(w_in, [int(p) for p in pts], axis=1)
    pad = jnp.zeros((w_in.shape[0], Z_WIDTH - Z_GLOW - 2 * GLA_GATE_RANK), w_in.dtype)
    return jnp.concatenate([aq, gv, gr, ga, gb, gq, gk, ak, av, lf, lb, pad], axis=1).astype(BF16)


def _layer(xp, xs, c_pad, seq_lens, w_ada, b_ada, w_in, attn_sink, gla_w2_fwd, gla_b_fwd, gla_w2_bwd, gla_b_bwd,
           gla_norm_w, w_branch_attn, w_branch_gla, w_out, ln1_g, ln1_b, peer_w_query, peer_subkeys,
           peer_u, peer_v, ln2_g, ln2_b):
    d = xp.shape[1]
    gcd_len = functools.reduce(math.gcd, seq_lens)
    tm_in = _pick(gcd_len, (512, 256, 128))
    tm_merge = _pick(gcd_len, (256, 128))
    tm_router = _pick(gcd_len, (256, 128))
    tm_peer = _pick(gcd_len, (512, 256, 128))

    mod = _ada(c_pad, w_ada, b_ada)
    mod3 = mod.reshape(c_pad.shape[0] * 6, 1, d)

    seq_in = _seq_tables(seq_lens, tm_in)[0]
    z = _ln_win(xp, xs, mod3, _reorder_w_in(w_in), seq_in, tm_in, Z_WIDTH // 7)

    _, blk_first, blk_last, blk_pos = _seq_tables(seq_lens, WINDOW_BLOCK)
    half = HEAD_DIM // 2
    inv = ROPE_THETA ** (-jnp.arange(half, dtype=F32) / half)
    ang = jnp.arange(max(seq_lens), dtype=F32)[:, None] * inv[None, :]
    cosf = jnp.concatenate([jnp.cos(ang), jnp.cos(ang)], axis=1)
    sinf = jnp.concatenate([-jnp.sin(ang), jnp.sin(ang)], axis=1)
    o_attn = _attention(z, attn_sink.astype(F32), cosf, sinf, blk_first + 2 * blk_last, blk_pos)

    _, ch_first, ch_last, _ = _seq_tables(seq_lens, GLA_CHUNK)
    qtf, kdf, qtb, kdb, decf, decb, o_intra = _gla_intra(
        z, gla_w2_fwd.astype(BF16), gla_b_fwd.reshape(1, GLA_K), gla_w2_bwd.astype(BF16), gla_b_bwd.reshape(1, GLA_K))
    o_f, o_b = _gla_state(z, qtf, kdf, decf, qtb, kdb, decb, o_intra, ch_first, ch_last)

    seq_merge = _seq_tables(seq_lens, tm_merge)[0]
    row = lambda v: v.reshape(1, d)
    x1, h2 = _merge(o_attn, o_f, o_b, z, xp, xs, mod3, row(gla_norm_w), w_branch_attn.astype(BF16),
                    w_branch_gla.astype(BF16), w_out.astype(BF16), row(ln1_g), row(ln1_b), seq_merge, tm_merge)

    rk, e2, cnt, coef = _router(h2, peer_w_query.astype(BF16), peer_subkeys.astype(BF16), tm_router)

    seq_peer = _seq_tables(seq_lens, tm_peer)[0]
    tn_peer = 1024
    return _peer(h2, peer_u.astype(BF16), _cast_transpose(peer_v, tn_peer), rk, e2, cnt, coef, x1, mod3,
                 row(ln2_g), row(ln2_b), seq_peer, tm_peer, tn_peer, xp.shape[0])


def kernel(x_prompt, x_sample, c_prompt, c_sample, w_ada, b_ada, w_in, attn_sink, gla_w2_fwd, gla_b_fwd,
           gla_w2_bwd, gla_b_bwd, gla_norm_w, w_branch_attn, w_branch_gla, w_out, ln1_g, ln1_b, peer_w_query,
           peer_subkeys, peer_u, peer_v, ln2_g, ln2_b):
    assert w_ada.shape[0] == 1, "single-layer trunk"
    bp, sp, d = x_prompt.shape
    bs, ss, _ = x_sample.shape
    seq_lens = [sp] * bp + [ss] * bs
    nseq = bp + bs
    c_pad = jnp.zeros((-(-nseq // 8) * 8, d), F32).at[:nseq].set(jnp.concatenate([c_prompt, c_sample], axis=0))
    yp, ys = _layer(x_prompt.reshape(bp * sp, d), x_sample.reshape(bs * ss, d), c_pad, seq_lens, w_ada[0], b_ada[0],
                    w_in[0], attn_sink[0], gla_w2_fwd[0], gla_b_fwd[0], gla_w2_bwd[0], gla_b_bwd[0], gla_norm_w[0],
                    w_branch_attn[0], w_branch_gla[0], w_out[0], ln1_g[0], ln1_b[0], peer_w_query[0],
                    peer_subkeys[0], peer_u[0], peer_v[0], ln2_g[0], ln2_b[0])
    return (yp.reshape(bp, sp, d), ys.reshape(bs, ss, d))
```

```python
import functools
import math

import numpy as np
import jax
import jax.numpy as jnp
from jax import lax
from jax.experimental import pallas as pl
from jax.experimental.pallas import tpu as pltpu

F32 = jnp.float32
BF16 = jnp.bfloat16

D_MODEL = 2048
ATT_HEADS = 16
ATT_KV_HEADS = 4
ATT_GROUP = ATT_HEADS // ATT_KV_HEADS
HEAD_DIM = 128
WINDOW_BLOCK = 128
ROPE_THETA = 10000.0
GLA_HEADS = 4
GLA_DK = 256
GLA_DV = 512
GLA_GATE_RANK = 16
GLA_TAU = 16.0
PEER_HEADS = 8
PEER_NKEYS = 128
PEER_N = PEER_NKEYS * PEER_NKEYS
PEER_TOPK = 16
DN_ALPHA = 2.0 ** 0.25
LN_EPS = 1e-5
RMS_EPS = 1e-6

ATT_Q = ATT_HEADS * HEAD_DIM
ATT_KV = ATT_KV_HEADS * HEAD_DIM
GLA_K = GLA_HEADS * GLA_DK
GLA_V = GLA_HEADS * GLA_DV
IN_WIDTHS = (ATT_Q, ATT_KV, ATT_KV, GLA_K, GLA_K, GLA_V, GLA_GATE_RANK, GLA_GATE_RANK, GLA_V, D_MODEL, D_MODEL)

Z_AQ = 0
Z_GV = 2048
Z_GR = 4096
Z_GA = 6144
Z_GB = 8192
Z_GQ = 10240
Z_GK = 11264
Z_AK = 12288
Z_AV = 12800
Z_GLOW = 13312
Z_WIDTH = 13440
LN_WIN_COL_TILES = 5

LANES = 128
PACK = 16
VMEM_LIMIT = 56 * 1024 * 1024

NEG_BIG = -1e30


def _ln(x):
    mu = jnp.mean(x, axis=-1, keepdims=True)
    xc = x - mu
    var = jnp.mean(xc * xc, axis=-1, keepdims=True)
    return xc * lax.rsqrt(var + LN_EPS)


def _dot_nt(a, b):
    return lax.dot_general(a, b, (((1,), (1,)), ((), ())), preferred_element_type=F32)


def _dot_tn(a, b):
    return lax.dot_general(a, b, (((0,), (0,)), ((), ())), preferred_element_type=F32)


def _dot(a, b):
    return jnp.dot(a, b, preferred_element_type=F32)


def _params(*sem, flags=None, vmem=VMEM_LIMIT):
    return pltpu.CompilerParams(dimension_semantics=sem, vmem_limit_bytes=vmem, flags=flags)


def _ada_kernel(c_ref, w_ref, b_ref, o_ref):
    c = c_ref[...]
    a = (c * jax.nn.sigmoid(c)).astype(BF16)
    o_ref[...] = _dot(a, w_ref[...].astype(BF16)) + b_ref[...]


def _ada(c_pad, w_ada, b_ada, tn=1024):
    rows, d = c_pad.shape
    n = w_ada.shape[1]
    return pl.pallas_call(
        _ada_kernel,
        grid=(n // tn,),
        in_specs=[
            pl.BlockSpec((rows, d), lambda j: (0, 0)),
            pl.BlockSpec((d, tn), lambda j: (0, j)),
            pl.BlockSpec((1, tn), lambda j: (0, j)),
        ],
        out_specs=pl.BlockSpec((rows, tn), lambda j: (0, j)),
        out_shape=jax.ShapeDtypeStruct((rows, n), F32),
        compiler_params=_params("arbitrary"),
        name="ada_mod",
    )(c_pad, w_ada, b_ada.reshape(1, n))


def _two_group_specs(tm, d, n_prompt_tiles, mode=None):
    pm = lambda i, *_: (jnp.minimum(i, n_prompt_tiles - 1), 0)
    sm = lambda i, *_: (jnp.maximum(i - n_prompt_tiles, 0), 0)
    return pl.BlockSpec((tm, d), pm, pipeline_mode=mode), pl.BlockSpec((tm, d), sm, pipeline_mode=mode)


def _lnwin_kernel(seq_ref, xp_ref, xs_ref, sh_ref, sc_ref, w_ref, o_ref, h_ref, *, n_prompt_tiles):
    i = pl.program_id(0)

    def fill(x_ref):
        h_ref[...] = (_ln(x_ref[...]) * (1.0 + sc_ref[0]) + sh_ref[0]).astype(BF16)

    @pl.when((pl.program_id(1) == 0) & (i < n_prompt_tiles))
    def _():
        fill(xp_ref)

    @pl.when((pl.program_id(1) == 0) & (i >= n_prompt_tiles))
    def _():
        fill(xs_ref)

    o_ref[...] = _dot(h_ref[...], w_ref[...]).astype(o_ref.dtype)


def _ln_win(xp, xs, mod3, w_in_r, tile_seq, tm, tn):
    d = xp.shape[1]
    t = xp.shape[0] + xs.shape[0]
    npt = xp.shape[0] // tm
    n = w_in_r.shape[1]
    pspec, sspec = _two_group_specs(tm, d, npt)
    grid_spec = pltpu.PrefetchScalarGridSpec(
        num_scalar_prefetch=1,
        grid=(t // tm, n // tn),
        in_specs=[
            pspec, sspec,
            pl.BlockSpec((1, 1, d), lambda i, j, s: (s[i] * 6 + 0, 0, 0)),
            pl.BlockSpec((1, 1, d), lambda i, j, s: (s[i] * 6 + 1, 0, 0)),
            pl.BlockSpec((d, tn), lambda i, j, s: (0, j)),
        ],
        out_specs=pl.BlockSpec((tm, tn), lambda i, j, s: (i, j)),
        scratch_shapes=[pltpu.VMEM((tm, d), BF16)],
    )
    return pl.pallas_call(
        functools.partial(_lnwin_kernel, n_prompt_tiles=npt),
        grid_spec=grid_spec,
        out_shape=jax.ShapeDtypeStruct((t, n), BF16),
        compiler_params=_params("arbitrary", "arbitrary"),
        name="ln_win",
    )(tile_seq, xp, xs, mod3, mod3, w_in_r)


def _rope(t, cosf, sinf):
    return t * cosf + pltpu.roll(t, HEAD_DIM // 2, 1) * sinf


def _attn_bias():
    blk = WINDOW_BLOCK
    r = (np.arange(ATT_GROUP * blk) % blk)[:, None]
    c = np.arange(3 * blk)[None, :]
    band = (c - r >= 0) & (c - r <= 2 * blk)
    out = []
    for var in range(4):
        lo = blk if var & 1 else 0
        hi = 2 * blk if var & 2 else 3 * blk
        out.append(np.where(band & (c >= lo) & (c < hi), 0.0, NEG_BIG))
    return jnp.asarray(np.stack(out).astype(np.float32))


def _attn_kernel(var_ref, pblk_ref, sink_ref, bias_ref, q_ref, kvp_ref, kvc_ref, kvn_ref,
                 tp_ref, tc_ref, tn_ref, o_ref):
    blk = WINDOW_BLOCK
    rows = ATT_GROUP * blk
    bias = bias_ref[0]
    grp = lax.broadcasted_iota(jnp.int32, (rows, 1), 0) // blk
    tabs = [(t_ref[:, 0:HEAD_DIM], t_ref[:, HEAD_DIM:2 * HEAD_DIM]) for t_ref in (tp_ref, tc_ref, tn_ref)]
    cosc, sinc = tabs[1]
    scale = HEAD_DIM ** -0.5
    ones = jnp.ones((3 * blk, HEAD_DIM), BF16)
    for g in range(ATT_KV_HEADS):
        ks = slice(g * HEAD_DIM, (g + 1) * HEAD_DIM)
        vs = slice(ATT_KV + g * HEAD_DIM, ATT_KV + (g + 1) * HEAD_DIM)
        kband = jnp.concatenate(
            [_rope(kv_ref[:, ks].astype(F32), cos_t, sin_t).astype(BF16)
             for kv_ref, (cos_t, sin_t) in zip((kvp_ref, kvc_ref, kvn_ref), tabs)], axis=0)
        vband = jnp.concatenate(
            [jnp.concatenate([kvp_ref[:, vs], kvc_ref[:, vs], kvn_ref[:, vs]], axis=0), ones], axis=1)
        qs = []
        sink_col = jnp.zeros((rows, 1), F32)
        for j in range(ATT_GROUP):
            h = g * ATT_GROUP + j
            qh = q_ref[:, h * HEAD_DIM:(h + 1) * HEAD_DIM].astype(F32)
            qs.append((_rope(qh, cosc, sinc) * scale).astype(BF16))
            sink_col = jnp.where(grp == j, sink_ref[h], sink_col)
        qg = jnp.concatenate(qs, axis=0)
        s = _dot_nt(qg, kband) + bias
        m = jnp.maximum(jnp.max(s, axis=-1, keepdims=True), sink_col)
        p = jnp.exp(s - m)
        ov = _dot(p.astype(BF16), vband)
        den = ov[:, HEAD_DIM:HEAD_DIM + 1] + jnp.exp(sink_col - m)
        o = ov[:, 0:HEAD_DIM] / den
        for j in range(ATT_GROUP):
            h = g * ATT_GROUP + j
            o_ref[:, h * HEAD_DIM:(h + 1) * HEAD_DIM] = o[j * blk:(j + 1) * blk].astype(o_ref.dtype)


def _rope_table(npos):
    half = HEAD_DIM // 2
    inv = ROPE_THETA ** (-np.arange(half, dtype=np.float64) / half)
    ang = np.arange(npos, dtype=np.float64)[:, None] * inv[None, :]
    cos, sin = np.cos(ang), np.sin(ang)
    return jnp.asarray(np.concatenate([cos, cos, -sin, sin], axis=1).astype(np.float32))


def _attention(z, sink, rope_tab, blk_var, blk_pos):
    t = z.shape[0]
    rows = ATT_GROUP * WINDOW_BLOCK
    blk = WINDOW_BLOCK
    nblk = t // blk
    npos = rope_tab.shape[0] // blk
    assert Z_AV == Z_AK + ATT_KV, "k and v heads are read as one column block"
    kvcol = Z_AK // (2 * ATT_KV)
    prev = lambda n: jnp.maximum(n - 1, 0)
    nxt = lambda n: jnp.minimum(n + 1, nblk - 1)
    pprev = lambda p: jnp.maximum(p - 1, 0)
    pnxt = lambda p: jnp.minimum(p + 1, npos - 1)
    tab = lambda f: pl.BlockSpec((blk, 2 * HEAD_DIM), lambda n, v, p: (f(p[n]), 0))
    same = lambda p: p
    grid_spec = pltpu.PrefetchScalarGridSpec(
        num_scalar_prefetch=2,
        grid=(nblk,),
        in_specs=[
            pl.BlockSpec(memory_space=pltpu.SMEM),
            pl.BlockSpec((1, rows, 3 * blk), lambda n, v, p: (v[n], 0, 0)),
            pl.BlockSpec((blk, ATT_Q), lambda n, v, p: (n, Z_AQ // ATT_Q)),
            pl.BlockSpec((blk, 2 * ATT_KV), lambda n, v, p: (prev(n), kvcol)),
            pl.BlockSpec((blk, 2 * ATT_KV), lambda n, v, p: (n, kvcol)),
            pl.BlockSpec((blk, 2 * ATT_KV), lambda n, v, p: (nxt(n), kvcol)),
            tab(pprev), tab(same), tab(pnxt),
        ],
        out_specs=pl.BlockSpec((blk, ATT_Q), lambda n, v, p: (n, 0)),
    )
    return pl.pallas_call(
        _attn_kernel,
        grid_spec=grid_spec,
        out_shape=jax.ShapeDtypeStruct((t, ATT_Q), BF16),
        compiler_params=_params("arbitrary"),
        name="win_attn",
    )(blk_var, blk_pos, sink, _attn_bias(), z, z, z, z, rope_tab, rope_tab, rope_tab)


GLA_CHUNK = 128
GLA_SUB = 16
GLA_NSUB = GLA_CHUNK // GLA_SUB


def _logsig(z):
    return -(jnp.maximum(-z, 0.0) + jnp.log1p(jnp.exp(-jnp.abs(z))))


def _mask_sums(masks, g):
    g_hi = g.astype(BF16)
    g_lo = (g - g_hi.astype(F32)).astype(BF16)
    return _dot(masks, g_hi) + _dot(masks, g_lo)


def _gla_intra_kernel(q_ref, k_ref, v_ref, l_ref, w2f_ref, bf_ref, w2b_ref, bb_ref,
                      qtf_ref, kdf_ref, qtb_ref, kdb_ref, decf_ref, decb_ref, oi_ref):
    c, sub, nsub = GLA_CHUNK, GLA_SUB, GLA_NSUB
    one = lambda m: jnp.where(m, 1.0, 0.0).astype(BF16)
    ri = lax.broadcasted_iota(jnp.int32, (c, c), 0)
    ci = lax.broadcasted_iota(jnp.int32, (c, c), 1)
    blk0 = (ri // sub) * sub
    bi = lax.broadcasted_iota(jnp.int32, (PACK, c), 0) * sub
    cj = lax.broadcasted_iota(jnp.int32, (PACK, c), 1)
    masks_f = jnp.concatenate([one(ci <= ri), one((ci > blk0) & (ci <= ri)), one(cj <= bi)], axis=0)
    masks_b = jnp.concatenate(
        [one(ci >= ri), one((ci >= ri) & (ci < blk0 + sub - 1)), one(cj >= bi + sub - 1)], axis=0)

    glow = l_ref[...]
    gf = _logsig(_dot(glow[:, 0:GLA_GATE_RANK], w2f_ref[...]) + bf_ref[...]) * (1.0 / GLA_TAU)
    gb = _logsig(_dot(glow[:, GLA_GATE_RANK:2 * GLA_GATE_RANK], w2b_ref[...]) + bb_ref[...]) * (1.0 / GLA_TAU)
    sums_f = _mask_sums(masks_f, gf)
    sums_b = _mask_sums(masks_b, gb)
    q = q_ref[...].astype(F32) * (GLA_DK ** -0.5)
    k = k_ref[...].astype(F32)

    rr = lax.broadcasted_iota(jnp.int32, (sub, c), 0)
    cc = lax.broadcasted_iota(jnp.int32, (sub, c), 1)
    zero_blk = jnp.zeros((sub, GLA_DK), BF16)

    def direction(sums, reverse, qt_ref, kd_ref, dec_ref):
        b, win, ref = sums[0:c], sums[c:2 * c], sums[2 * c:2 * c + PACK]
        edge = b[0:1] if reverse else b[c - 1:c]
        qt_ref[...] = (q * jnp.exp(b)).astype(BF16)
        kd_ref[...] = (k * jnp.exp(edge - b)).astype(BF16)
        dec_ref[0] = jnp.exp(edge)
        qw = (q * jnp.exp(win)).astype(BF16)
        kw = k * jnp.exp(-win)
        att = [[] for _ in range(GLA_HEADS)]
        for i in range(nsub):
            dmat = jnp.exp(jnp.minimum(ref[i:i + 1] - ref, 0.0))
            live = range(i, nsub) if reverse else range(0, i + 1)
            for hd in range(GLA_HEADS):
                sl = slice(hd * GLA_DK, (hd + 1) * GLA_DK)
                kh = jnp.concatenate(
                    [(kw[j * sub:(j + 1) * sub, sl] * dmat[j:j + 1, sl]).astype(BF16) if j in live else zero_blk
                     for j in range(nsub)], axis=0)
                a = _dot_nt(qw[i * sub:(i + 1) * sub, sl], kh)
                keep = (cc >= rr + i * sub) if reverse else (cc <= rr + i * sub)
                att[hd].append(jnp.where(keep, a, 0.0))
        return [jnp.concatenate(rows, axis=0) for rows in att]

    att_f = direction(sums_f, False, qtf_ref, kdf_ref, decf_ref)
    att_b = direction(sums_b, True, qtb_ref, kdb_ref, decb_ref)
    for hd in range(GLA_HEADS):
        vsl = slice(hd * GLA_DV, (hd + 1) * GLA_DV)
        oi_ref[:, vsl] = _dot((att_f[hd] + att_b[hd]).astype(BF16), v_ref[:, vsl]).astype(oi_ref.dtype)


def _gla_intra(z, w2f, bf, w2b, bb):
    t = z.shape[0]
    c = GLA_CHUNK
    nch = t // c
    zspec = lambda w, off: pl.BlockSpec((c, w), lambda n: (n, off // w))
    wspec = pl.BlockSpec((GLA_GATE_RANK, GLA_K), lambda n: (0, 0))
    bspec = pl.BlockSpec((1, GLA_K), lambda n: (0, 0))
    tok = lambda w: pl.BlockSpec((c, w), lambda n: (n, 0))
    dec = pl.BlockSpec((1, 1, GLA_K), lambda n: (n, 0, 0))
    tk = jax.ShapeDtypeStruct((t, GLA_K), BF16)
    dk = jax.ShapeDtypeStruct((nch, 1, GLA_K), F32)
    return pl.pallas_call(
        _gla_intra_kernel,
        grid=(nch,),
        in_specs=[zspec(GLA_K, Z_GQ), zspec(GLA_K, Z_GK), zspec(GLA_V, Z_GV), zspec(LANES, Z_GLOW),
                  wspec, bspec, wspec, bspec],
        out_specs=[tok(GLA_K), tok(GLA_K), tok(GLA_K), tok(GLA_K), dec, dec, tok(GLA_V)],
        out_shape=[tk, tk, tk, tk, dk, dk, jax.ShapeDtypeStruct((t, GLA_V), BF16)],
        compiler_params=_params("arbitrary"),
        name="gla_intra",
    )(z, z, z, z, w2f, bf, w2b, bb)


def _gla_state_kernel(first_ref, last_ref, qf_ref, kf_ref, vf_ref, df_ref, qb_ref, kb_ref, vb_ref, db_ref, oi_ref,
                      of_ref, ob_ref, sf_ref, sb_ref):
    n = pl.program_id(0)
    nch = pl.num_programs(0)

    @pl.when(first_ref[n] == 1)
    def _():
        sf_ref[...] = jnp.zeros_like(sf_ref)

    @pl.when(last_ref[nch - 1 - n] == 1)
    def _():
        sb_ref[...] = jnp.zeros_like(sb_ref)

    for hd in range(GLA_HEADS):
        ksl = slice(hd * GLA_DK, (hd + 1) * GLA_DK)
        vsl = slice(hd * GLA_DV, (hd + 1) * GLA_DV)
        for q_ref, k_ref, v_ref, d_ref, o_ref, s_ref in (
                (qf_ref, kf_ref, vf_ref, df_ref, of_ref, sf_ref), (qb_ref, kb_ref, vb_ref, db_ref, ob_ref, sb_ref)):
            state = s_ref[hd]
            inter = _dot_nt(q_ref[:, ksl], state.astype(BF16))
            if o_ref is of_ref:
                inter = inter + oi_ref[:, vsl].astype(F32)
            o_ref[:, vsl] = inter.astype(o_ref.dtype)
            s_ref[hd] = d_ref[0][:, ksl] * state + _dot_tn(v_ref[:, vsl], k_ref[:, ksl])


def _gla_state(z, qtf, kdf, decf, qtb, kdb, decb, o_intra, ch_first, ch_last):
    t = z.shape[0]
    c = GLA_CHUNK
    nch = t // c
    fwd = lambda n: n
    bwd = lambda n: nch - 1 - n
    tok = lambda f: pl.BlockSpec((c, GLA_K), lambda n, a, b: (f(n), 0))
    val = lambda f: pl.BlockSpec((c, GLA_V), lambda n, a, b: (f(n), Z_GV // GLA_V))
    dec = lambda f: pl.BlockSpec((1, 1, GLA_K), lambda n, a, b: (f(n), 0, 0))
    out = lambda f: pl.BlockSpec((c, GLA_V), lambda n, a, b: (f(n), 0))
    grid_spec = pltpu.PrefetchScalarGridSpec(
        num_scalar_prefetch=2,
        grid=(nch,),
        in_specs=[tok(fwd), tok(fwd), val(fwd), dec(fwd), tok(bwd), tok(bwd), val(bwd), dec(bwd), out(fwd)],
        out_specs=[out(fwd), out(bwd)],
        scratch_shapes=[pltpu.VMEM((GLA_HEADS, GLA_DV, GLA_DK), F32), pltpu.VMEM((GLA_HEADS, GLA_DV, GLA_DK), F32)],
    )
    o = jax.ShapeDtypeStruct((t, GLA_V), BF16)
    return pl.pallas_call(
        _gla_state_kernel,
        grid_spec=grid_spec,
        out_shape=[o, o],
        compiler_params=_params("arbitrary"),
        name="gla_state",
    )(ch_first, ch_last, qtf, kdf, z, decf, qtb, kdb, z, decb, o_intra)


def _merge_kernel(seq_ref, oa_ref, of_ref, ob_ref, gr_ref, ga_ref, gb_ref, xp_ref, xs_ref, nw_ref,
                  wa_ref, wb_ref, wo_ref, g1_ref, l1g_ref, l1b_ref, sh2_ref, sc2_ref, x1_ref, h2_ref,
                  *, n_prompt_tiles):
    i = pl.program_id(0)
    og = of_ref[...].astype(F32) + ob_ref[...].astype(F32)
    parts = []
    for h in range(GLA_HEADS):
        th = og[:, h * GLA_DV:(h + 1) * GLA_DV]
        ms = jnp.mean(th * th, axis=-1, keepdims=True)
        parts.append(th * lax.rsqrt(ms + RMS_EPS))
    gr = gr_ref[...].astype(F32)
    ogn = jnp.concatenate(parts, axis=1) * nw_ref[...] * (gr * jax.nn.sigmoid(gr))
    a = _dot(oa_ref[...], wa_ref[...])
    b = _dot(ogn.astype(BF16), wb_ref[...])
    merged = jax.nn.sigmoid(ga_ref[...].astype(F32)) * a + jax.nn.sigmoid(gb_ref[...].astype(F32)) * b
    mix = g1_ref[0] * _dot(merged.astype(BF16), wo_ref[...])

    def finish(x_ref):
        x1 = _ln(DN_ALPHA * x_ref[...] + mix) * l1g_ref[...] + l1b_ref[...]
        x1_ref[...] = x1
        h2_ref[...] = (_ln(x1) * (1.0 + sc2_ref[0]) + sh2_ref[0]).astype(BF16)

    @pl.when(i < n_prompt_tiles)
    def _():
        finish(xp_ref)

    @pl.when(i >= n_prompt_tiles)
    def _():
        finish(xs_ref)


def _merge(o_attn, o_f, o_b, z, xp, xs, mod3, norm_w, wa, wb, wo, ln1_g, ln1_b, tile_seq, tm):
    d = xp.shape[1]
    t = xp.shape[0] + xs.shape[0]
    npt = xp.shape[0] // tm
    tok = lambda col: pl.BlockSpec((tm, d), lambda i, s: (i, col))
    const = lambda shape: pl.BlockSpec(shape, lambda i, s: (0,) * len(shape), pipeline_mode=pl.Buffered(1))
    modspec = lambda k: pl.BlockSpec((1, 1, d), lambda i, s: (s[i] * 6 + k, 0, 0))
    pspec, sspec = _two_group_specs(tm, d, npt)
    grid_spec = pltpu.PrefetchScalarGridSpec(
        num_scalar_prefetch=1,
        grid=(t // tm,),
        in_specs=[tok(0), tok(0), tok(0), tok(Z_GR // d), tok(Z_GA // d), tok(Z_GB // d), pspec, sspec,
                  const((1, d)), const((d, d)), const((d, d)), const((d, d)),
                  modspec(2), const((1, d)), const((1, d)), modspec(3), modspec(4)],
        out_specs=[tok(0), tok(0)],
    )
    return pl.pallas_call(
        functools.partial(_merge_kernel, n_prompt_tiles=npt),
        grid_spec=grid_spec,
        out_shape=[jax.ShapeDtypeStruct((t, d), F32), jax.ShapeDtypeStruct((t, d), BF16)],
        compiler_params=_params("arbitrary"),
        name="merge_ln1",
    )(tile_seq, o_attn, o_f, o_b, z, z, z, xp, xs, norm_w, wa, wb, wo, mod3, ln1_g, ln1_b, mod3, mod3)


SLAB = 8
RANK_NONE = 127.0


def _slabs(s):
    return [s[i * SLAB:(i + 1) * SLAB] for i in range(s.shape[0] // SLAB)]


def _extract_desc(slabs, count):
    vals = []
    for _ in range(count):
        m8 = functools.reduce(jnp.maximum, slabs)
        m = jnp.max(m8, axis=0, keepdims=True)
        vals.append(m)
        slabs = [jnp.where(sl == m, -jnp.inf, sl) for sl in slabs]
    return vals


def _sorting_network(n):
    pairs = []
    p = 1
    while p < n:
        k = p
        while k >= 1:
            for j in range(k % p, n - k, 2 * k):
                for i in range(min(k, n - j - k)):
                    if (i + j) // (2 * p) == (i + j + k) // (2 * p):
                        pairs.append((i + j, i + j + k))
            k //= 2
        p *= 2
    return pairs


def _extract_sorted(slabs, count):
    cols = list(slabs)
    for i, j in _sorting_network(len(cols)):
        cols[i], cols[j] = jnp.maximum(cols[i], cols[j]), jnp.minimum(cols[i], cols[j])
    vals = []
    for it in range(count):
        m = jnp.max(cols[0], axis=0, keepdims=True)
        vals.append(m)
        hit = cols[0] == m
        live = count - 1 - it
        cols = [jnp.where(hit, cols[k + 1], cols[k]) for k in range(live)]
    return vals


def _stack_rows(vals, rows, lanes):
    ridx = lax.broadcasted_iota(jnp.int32, (rows, lanes), 0)
    acc = jnp.full((rows, lanes), -jnp.inf, F32)
    for i, v in enumerate(vals):
        acc = jnp.where(ridx == i, v, acc)
    return acc


def _router_kernel(h2_ref, wq_ref, sk_ref, rk_ref, e2_ref, cnt_ref, c_ref):
    tm = h2_ref.shape[0]
    kx = PEER_TOPK
    q = _dot(h2_ref[...], wq_ref[...]).astype(BF16)
    ridx = lax.broadcasted_iota(jnp.int32, (SLAB, tm), 0)
    for h in range(PEER_HEADS):
        s1 = _dot_nt(sk_ref[h, 0], q[:, (2 * h) * PEER_NKEYS:(2 * h + 1) * PEER_NKEYS])
        s2 = _dot_nt(sk_ref[h, 1], q[:, (2 * h + 1) * PEER_NKEYS:(2 * h + 2) * PEER_NKEYS])
        v1 = _extract_sorted(_slabs(s1), kx)
        v2 = _extract_sorted(_slabs(s2), kx)
        rank2 = jnp.full_like(s2, RANK_NONE)
        for b in reversed(range(kx)):
            rank2 = jnp.where(s2 >= v2[b], float(b), rank2)
        st1 = _stack_rows(v1, 2 * SLAB, tm)
        st2 = _stack_rows(v2, 2 * SLAB, tm)
        cand = []
        for a in range(SLAB):
            nb = kx // (a + 1)
            for sl in range((nb + SLAB - 1) // SLAB):
                piece = st2[sl * SLAB:(sl + 1) * SLAB] + v1[a]
                if nb < (sl + 1) * SLAB:
                    piece = jnp.where(ridx < nb - sl * SLAB, piece, -jnp.inf)
                cand.append(piece)
        cand.append(st1[SLAB:2 * SLAB] + v2[0])
        top = _extract_desc(cand, kx)
        zsum = functools.reduce(lambda acc, v: acc + jnp.exp(v - top[0]), top, jnp.zeros_like(top[0]))
        tau = top[kx - 1]
        cnt = jnp.zeros_like(s1)
        for b in range(SLAB):
            cnt = jnp.where(s1 + v2[b] >= tau, float(b + 1), cnt)
        best = jnp.full_like(v1[0], float(SLAB))
        for b in range(SLAB, kx):
            best = jnp.where(v1[0] + v2[b] >= tau, float(b + 1), best)
        cnt = jnp.where((s1 == v1[0]) & (cnt == float(SLAB)), best, cnt)
        packed = (PEER_NKEYS // PACK, PACK, tm)
        rk_ref[h] = rank2.astype(BF16).reshape(packed)
        e2_ref[h] = jnp.exp(s2 - v2[0]).astype(BF16).reshape(packed)
        cnt_ref[h] = cnt
        c_ref[h] = jnp.exp(s1 - v1[0]) / zsum


def _router(h2, wq, subkeys, tm):
    t, d = h2.shape
    nq = wq.shape[1]
    shape = (PEER_HEADS, PEER_NKEYS, t)
    ospec = pl.BlockSpec((PEER_HEADS, PEER_NKEYS, tm), lambda i: (0, 0, i))
    pshape = (PEER_HEADS, PEER_NKEYS // PACK, PACK, t)
    pspec = pl.BlockSpec((PEER_HEADS, PEER_NKEYS // PACK, PACK, tm), lambda i: (0, 0, 0, i))
    return pl.pallas_call(
        _router_kernel,
        grid=(t // tm,),
        in_specs=[
            pl.BlockSpec((tm, d), lambda i: (i, 0)),
            pl.BlockSpec((d, nq), lambda i: (0, 0), pipeline_mode=pl.Buffered(1)),
            pl.BlockSpec(subkeys.shape, lambda i: (0, 0, 0, 0), pipeline_mode=pl.Buffered(1)),
        ],
        out_specs=[pspec, pspec, ospec, ospec],
        out_shape=[jax.ShapeDtypeStruct(pshape, BF16), jax.ShapeDtypeStruct(pshape, BF16),
                   jax.ShapeDtypeStruct(shape, F32), jax.ShapeDtypeStruct(shape, F32)],
        compiler_params=_params("arbitrary"),
        name="peer_router",
    )(h2, wq, subkeys)


_SQRT_HALF = math.sqrt(0.5)


def _gelu(x):
    return 0.5 * x * (1.0 + lax.erf(x * _SQRT_HALF))


def _cast_t_kernel(x_ref, o_ref):
    o_ref[...] = x_ref[...].T.astype(o_ref.dtype)


def _cast_transpose(x, tn):
    n, d = x.shape
    return pl.pallas_call(
        _cast_t_kernel,
        grid=(n // tn,),
        in_specs=[pl.BlockSpec((tn, d), lambda j: (j, 0))],
        out_specs=pl.BlockSpec((None, d, tn), lambda j: (j, 0, 0)),
        out_shape=jax.ShapeDtypeStruct((n // tn, d, tn), BF16),
        compiler_params=_params("arbitrary"),
        name="cast_transpose",
    )(x)


def _peer_first_kernel(u_ref, h2_ref, o_ref):
    o_ref[...] = _dot_nt(u_ref[...], h2_ref[...])


def _peer_first(h2, u_bf, tm, tn):
    d = h2.shape[1]
    return pl.pallas_call(
        _peer_first_kernel,
        grid=(1,),
        in_specs=[pl.BlockSpec((tn, d), lambda i: (0, 0)), pl.BlockSpec((tm, d), lambda i: (0, 0))],
        out_specs=pl.BlockSpec((tn, tm), lambda i: (0, 0)),
        out_shape=jax.ShapeDtypeStruct((tn, tm), F32),
        compiler_params=_params("arbitrary"),
        name="peer_first",
    )(u_bf, h2)


def _peer_kernel(seq_ref, h2n_ref, un_ref, vt_ref, rk_ref, e2_ref, cnt_ref, c_ref, at_init_ref, x1_ref, g2_ref,
                 l2g_ref, l2b_ref, op_ref, os_ref, at0_ref, at1_ref, wt_ref, acc_ref, *, n_prompt_tiles):
    i = pl.program_id(0)
    j = pl.program_id(1)
    tm = h2n_ref.shape[0]
    ni = un_ref.shape[0] // PEER_NKEYS
    groups = PEER_NKEYS // PACK

    @pl.when((i == 0) & (j == 0))
    def _():
        at0_ref[...] = at_init_ref[...]

    @pl.when(j == 0)
    def _():
        acc_ref[...] = jnp.zeros_like(acc_ref)

    def step(at_cur, at_next):
        at_next[...] = _dot_nt(un_ref[...], h2n_ref[...])
        for ii in range(ni):
            rows = slice(ii * PEER_NKEYS, (ii + 1) * PEER_NKEYS)
            gate = None
            for h in range(PEER_HEADS):
                cnt = jnp.broadcast_to(cnt_ref[h, ii:ii + 1, :], (PACK, tm)).astype(BF16)
                coef = jnp.broadcast_to(c_ref[h, ii:ii + 1, :], (PACK, tm)).astype(BF16)
                sel = jnp.where(rk_ref[h] < cnt[None], e2_ref[h] * coef[None], jnp.zeros((), BF16))
                gate = sel if gate is None else gate + sel
            act = _gelu(at_cur[rows, :]).astype(BF16)
            wt_ref[rows, :] = gate.reshape(PEER_NKEYS, tm) * act
        acc_ref[...] += _dot(vt_ref[...], wt_ref[...])

    @pl.when(j % 2 == 0)
    def _():
        step(at0_ref, at1_ref)

    @pl.when(j % 2 == 1)
    def _():
        step(at1_ref, at0_ref)

    @pl.when(j == pl.num_programs(1) - 1)
    def _():
        ff = acc_ref[...].T
        y = _ln(DN_ALPHA * x1_ref[...] + g2_ref[0] * ff) * l2g_ref[...] + l2b_ref[...]

        @pl.when(i < n_prompt_tiles)
        def _():
            op_ref[...] = y

        @pl.when(i >= n_prompt_tiles)
        def _():
            os_ref[...] = y


def _peer(h2, u_bf, vt_bf, rk, e2, cnt, coef, x1, mod3, ln2_g, ln2_b, tile_seq, tm, tn, t_prompt):
    t, d = x1.shape
    ne = u_bf.shape[0]
    ni = tn // PEER_NKEYS
    nti, ntj = t // tm, ne // tn
    assert ntj % 2 == 0, "the two A^T buffers alternate by the parity of the expert-tile index"
    npt = t_prompt // tm
    groups = PEER_NKEYS // PACK
    at_init = _peer_first(h2, u_bf, tm, tn)
    nxt_i = lambda i, j: jnp.minimum(jnp.where(j == ntj - 1, i + 1, i), nti - 1)
    nxt_j = lambda j: jnp.where(j == ntj - 1, 0, j + 1)
    once = pl.Buffered(1)
    const = lambda shape: pl.BlockSpec(shape, lambda i, j, s: (0,) * len(shape), pipeline_mode=once)
    grid_spec = pltpu.PrefetchScalarGridSpec(
        num_scalar_prefetch=1,
        grid=(nti, ntj),
        in_specs=[
            pl.BlockSpec((tm, d), lambda i, j, s: (nxt_i(i, j), 0), pipeline_mode=once),
            pl.BlockSpec((tn, d), lambda i, j, s: (nxt_j(j), 0)),
            pl.BlockSpec((None, d, tn), lambda i, j, s: (j, 0, 0)),
            pl.BlockSpec((PEER_HEADS, groups, PACK, tm), lambda i, j, s: (0, 0, 0, i), pipeline_mode=once),
            pl.BlockSpec((PEER_HEADS, groups, PACK, tm), lambda i, j, s: (0, 0, 0, i), pipeline_mode=once),
            pl.BlockSpec((PEER_HEADS, ni, tm), lambda i, j, s: (0, j, i)),
            pl.BlockSpec((PEER_HEADS, ni, tm), lambda i, j, s: (0, j, i)),
            const((tn, tm)),
            pl.BlockSpec((tm, d), lambda i, j, s: (i, 0), pipeline_mode=once),
            pl.BlockSpec((1, 1, d), lambda i, j, s: (s[i] * 6 + 5, 0, 0)),
            const((1, d)), const((1, d)),
        ],
        out_specs=[pl.BlockSpec((tm, d), lambda i, j, s: (jnp.minimum(i, npt - 1), 0)),
                   pl.BlockSpec((tm, d), lambda i, j, s: (jnp.maximum(i - npt, 0), 0))],
        scratch_shapes=[pltpu.VMEM((tn, tm), F32), pltpu.VMEM((tn, tm), F32), pltpu.VMEM((tn, tm), BF16),
                        pltpu.VMEM((d, tm), F32)],
    )
    return pl.pallas_call(
        functools.partial(_peer_kernel, n_prompt_tiles=npt),
        grid_spec=grid_spec,
        out_shape=[jax.ShapeDtypeStruct((t_prompt, d), F32), jax.ShapeDtypeStruct((t - t_prompt, d), F32)],
        compiler_params=_params("arbitrary", "arbitrary"),
        name="peer_dense",
    )(tile_seq, h2, u_bf, vt_bf, rk, e2, cnt, coef, at_init, x1, mod3, ln2_g, ln2_b)


def _seq_tables(seq_lens, unit):
    sid, first, last, pos = [], [], [], []
    for s, n in enumerate(seq_lens):
        k = n // unit
        sid += [s] * k
        first += [1] + [0] * (k - 1)
        last += [0] * (k - 1) + [1]
        pos += list(range(k))
    mk = lambda v: jnp.asarray(np.asarray(v, np.int32))
    return mk(sid), mk(first), mk(last), mk(pos)


def _pick(n, options):
    for o in options:
        if n % o == 0:
            return o
    raise ValueError(f"no tile size in {options} divides {n}")


def _reorder_w_in(w_in):
    pts = np.cumsum(IN_WIDTHS)[:-1]
    aq, ak, av, gq, gk, gv, lf, lb, gr, ga, gb = jnp.split(w_in, [int(p) for p in pts], axis=1)
    pad = jnp.zeros((w_in.shape[0], Z_WIDTH - Z_GLOW - 2 * GLA_GATE_RANK), w_in.dtype)
    return jnp.concatenate([s.astype(BF16) for s in (aq, gv, gr, ga, gb, gq, gk, ak, av, lf, lb, pad)], axis=1)


def _layer(xp, xs, c_pad, seq_lens, w_ada, b_ada, w_in, attn_sink, gla_w2_fwd, gla_b_fwd, gla_w2_bwd, gla_b_bwd,
           gla_norm_w, w_branch_attn, w_branch_gla, w_out, ln1_g, ln1_b, peer_w_query, peer_subkeys,
           peer_u, peer_v, ln2_g, ln2_b):
    d = xp.shape[1]
    gcd_len = functools.reduce(math.gcd, seq_lens)
    tm_in = _pick(gcd_len, (512, 256, 128))
    tm_merge = _pick(gcd_len, (256, 128))
    tm_router = _pick(gcd_len, (256, 128))
    tm_peer = _pick(gcd_len, (512, 256, 128))

    mod = _ada(c_pad, w_ada, b_ada)
    mod3 = mod.reshape(c_pad.shape[0] * 6, 1, d)

    seq_in = _seq_tables(seq_lens, tm_in)[0]
    z = _ln_win(xp, xs, mod3, _reorder_w_in(w_in), seq_in, tm_in, Z_WIDTH // LN_WIN_COL_TILES)

    _, blk_first, blk_last, blk_pos = _seq_tables(seq_lens, WINDOW_BLOCK)
    o_attn = _attention(z, attn_sink.astype(F32), _rope_table(max(seq_lens)), blk_first + 2 * blk_last, blk_pos)

    _, ch_first, ch_last, _ = _seq_tables(seq_lens, GLA_CHUNK)
    qtf, kdf, qtb, kdb, decf, decb, o_intra = _gla_intra(
        z, gla_w2_fwd.astype(BF16), gla_b_fwd.reshape(1, GLA_K), gla_w2_bwd.astype(BF16), gla_b_bwd.reshape(1, GLA_K))
    o_f, o_b = _gla_state(z, qtf, kdf, decf, qtb, kdb, decb, o_intra, ch_first, ch_last)

    seq_merge = _seq_tables(seq_lens, tm_merge)[0]
    row = lambda v: v.reshape(1, d)
    x1, h2 = _merge(o_attn, o_f, o_b, z, xp, xs, mod3, row(gla_norm_w), w_branch_attn.astype(BF16),
                    w_branch_gla.astype(BF16), w_out.astype(BF16), row(ln1_g), row(ln1_b), seq_merge, tm_merge)

    rk, e2, cnt, coef = _router(h2, peer_w_query.astype(BF16), peer_subkeys.astype(BF16), tm_router)

    seq_peer = _seq_tables(seq_lens, tm_peer)[0]
    tn_peer = 1024
    return _peer(h2, peer_u.astype(BF16), _cast_transpose(peer_v, tn_peer), rk, e2, cnt, coef, x1, mod3,
                 row(ln2_g), row(ln2_b), seq_peer, tm_peer, tn_peer, xp.shape[0])


def kernel(x_prompt, x_sample, c_prompt, c_sample, w_ada, b_ada, w_in, attn_sink, gla_w2_fwd, gla_b_fwd,
           gla_w2_bwd, gla_b_bwd, gla_norm_w, w_branch_attn, w_branch_gla, w_out, ln1_g, ln1_b, peer_w_query,
           peer_subkeys, peer_u, peer_v, ln2_g, ln2_b):
    assert w_ada.shape[0] == 1, "single-layer trunk"
    bp, sp, d = x_prompt.shape
    bs, ss, _ = x_sample.shape
    seq_lens = [sp] * bp + [ss] * bs
    nseq = bp + bs
    c_pad = jnp.zeros((-(-nseq // 8) * 8, d), F32).at[:nseq].set(jnp.concatenate([c_prompt, c_sample], axis=0))
    yp, ys = _layer(x_prompt.reshape(bp * sp, d), x_sample.reshape(bs * ss, d), c_pad, seq_lens, w_ada[0], b_ada[0],
                    w_in[0], attn_sink[0], gla_w2_fwd[0], gla_b_fwd[0], gla_w2_bwd[0], gla_b_bwd[0], gla_norm_w[0],
                    w_branch_attn[0], w_branch_gla[0], w_out[0], ln1_g[0], ln1_b[0], peer_w_query[0],
                    peer_subkeys[0], peer_u[0], peer_v[0], ln2_g[0], ln2_b[0])
    return (yp.reshape(bp, sp, d), ys.reshape(bs, ss, d))
```

```python
import functools
import math

import numpy as np
import jax
import jax.numpy as jnp
from jax import lax
from jax.experimental import pallas as pl
from jax.experimental.pallas import tpu as pltpu

F32 = jnp.float32
BF16 = jnp.bfloat16

D_MODEL = 2048
ATT_HEADS = 16
ATT_KV_HEADS = 4
ATT_GROUP = ATT_HEADS // ATT_KV_HEADS
HEAD_DIM = 128
WINDOW_BLOCK = 128
ROPE_THETA = 10000.0
GLA_HEADS = 4
GLA_DK = 256
GLA_DV = 512
GLA_GATE_RANK = 16
GLA_TAU = 16.0
PEER_HEADS = 8
PEER_NKEYS = 128
PEER_N = PEER_NKEYS * PEER_NKEYS
PEER_TOPK = 16
DN_ALPHA = 2.0 ** 0.25
LN_EPS = 1e-5
RMS_EPS = 1e-6

ATT_Q = ATT_HEADS * HEAD_DIM
ATT_KV = ATT_KV_HEADS * HEAD_DIM
GLA_K = GLA_HEADS * GLA_DK
GLA_V = GLA_HEADS * GLA_DV
IN_WIDTHS = (ATT_Q, ATT_KV, ATT_KV, GLA_K, GLA_K, GLA_V, GLA_GATE_RANK, GLA_GATE_RANK, GLA_V, D_MODEL, D_MODEL)

Z_AQ = 0
Z_GV = 2048
Z_GR = 4096
Z_GA = 6144
Z_GB = 8192
Z_GQ = 10240
Z_GK = 11264
Z_AK = 12288
Z_AV = 12800
Z_GLOW = 13312
Z_WIDTH = 13440
LN_WIN_COL_TILES = 5

LANES = 128
PACK = 16
VMEM_LIMIT = 56 * 1024 * 1024

NEG_BIG = -1e30


def _ln(x):
    mu = jnp.mean(x, axis=-1, keepdims=True)
    xc = x - mu
    var = jnp.mean(xc * xc, axis=-1, keepdims=True)
    return xc * lax.rsqrt(var + LN_EPS)


def _dot_nt(a, b):
    return lax.dot_general(a, b, (((1,), (1,)), ((), ())), preferred_element_type=F32)


def _dot_tn(a, b):
    return lax.dot_general(a, b, (((0,), (0,)), ((), ())), preferred_element_type=F32)


def _dot(a, b):
    return jnp.dot(a, b, preferred_element_type=F32)


def _params(*sem, flags=None, vmem=VMEM_LIMIT):
    return pltpu.CompilerParams(dimension_semantics=sem, vmem_limit_bytes=vmem, flags=flags)


def _ada_kernel(c_ref, w_ref, b_ref, o_ref):
    c = c_ref[...]
    a = (c * jax.nn.sigmoid(c)).astype(BF16)
    o_ref[...] = _dot(a, w_ref[...].astype(BF16)) + b_ref[...]


def _ada(c_pad, w_ada, b_ada, tn=1024):
    rows, d = c_pad.shape
    n = w_ada.shape[1]
    return pl.pallas_call(
        _ada_kernel,
        grid=(n // tn,),
        in_specs=[
            pl.BlockSpec((rows, d), lambda j: (0, 0)),
            pl.BlockSpec((d, tn), lambda j: (0, j)),
            pl.BlockSpec((1, tn), lambda j: (0, j)),
        ],
        out_specs=pl.BlockSpec((rows, tn), lambda j: (0, j)),
        out_shape=jax.ShapeDtypeStruct((rows, n), F32),
        compiler_params=_params("arbitrary"),
        name="ada_mod",
    )(c_pad, w_ada, b_ada.reshape(1, n))


def _two_group_specs(tm, d, n_prompt_tiles, mode=None):
    pm = lambda i, *_: (jnp.minimum(i, n_prompt_tiles - 1), 0)
    sm = lambda i, *_: (jnp.maximum(i - n_prompt_tiles, 0), 0)
    return pl.BlockSpec((tm, d), pm, pipeline_mode=mode), pl.BlockSpec((tm, d), sm, pipeline_mode=mode)


def _lnwin_kernel(seq_ref, xp_ref, xs_ref, sh_ref, sc_ref, w_ref, o_ref, h_ref, *, n_prompt_tiles):
    i = pl.program_id(0)

    def fill(x_ref):
        h_ref[...] = (_ln(x_ref[...]) * (1.0 + sc_ref[0]) + sh_ref[0]).astype(BF16)

    @pl.when((pl.program_id(1) == 0) & (i < n_prompt_tiles))
    def _():
        fill(xp_ref)

    @pl.when((pl.program_id(1) == 0) & (i >= n_prompt_tiles))
    def _():
        fill(xs_ref)

    o_ref[...] = _dot(h_ref[...], w_ref[...]).astype(o_ref.dtype)


def _ln_win(xp, xs, mod3, w_in_r, tile_seq, tm, tn):
    d = xp.shape[1]
    t = xp.shape[0] + xs.shape[0]
    npt = xp.shape[0] // tm
    n = w_in_r.shape[1]
    pspec, sspec = _two_group_specs(tm, d, npt)
    grid_spec = pltpu.PrefetchScalarGridSpec(
        num_scalar_prefetch=1,
        grid=(t // tm, n // tn),
        in_specs=[
            pspec, sspec,
            pl.BlockSpec((1, 1, d), lambda i, j, s: (s[i] * 6 + 0, 0, 0)),
            pl.BlockSpec((1, 1, d), lambda i, j, s: (s[i] * 6 + 1, 0, 0)),
            pl.BlockSpec((d, tn), lambda i, j, s: (0, j)),
        ],
        out_specs=pl.BlockSpec((tm, tn), lambda i, j, s: (i, j)),
        scratch_shapes=[pltpu.VMEM((tm, d), BF16)],
    )
    return pl.pallas_call(
        functools.partial(_lnwin_kernel, n_prompt_tiles=npt),
        grid_spec=grid_spec,
        out_shape=jax.ShapeDtypeStruct((t, n), BF16),
        compiler_params=_params("arbitrary", "arbitrary"),
        name="ln_win",
    )(tile_seq, xp, xs, mod3, mod3, w_in_r)


def _rope(t, cosf, sinf):
    return t * cosf + pltpu.roll(t, HEAD_DIM // 2, 1) * sinf


def _attn_bias():
    blk = WINDOW_BLOCK
    r = (np.arange(ATT_GROUP * blk) % blk)[:, None]
    c = np.arange(3 * blk)[None, :]
    band = (c - r >= 0) & (c - r <= 2 * blk)
    out = []
    for var in range(4):
        lo = blk if var & 1 else 0
        hi = 2 * blk if var & 2 else 3 * blk
        out.append(np.where(band & (c >= lo) & (c < hi), 0.0, NEG_BIG))
    return jnp.asarray(np.stack(out).astype(np.float32))


def _attn_kernel(var_ref, pblk_ref, sink_ref, bias_ref, q_ref, kvp_ref, kvc_ref, kvn_ref,
                 tp_ref, tc_ref, tn_ref, o_ref):
    blk = WINDOW_BLOCK
    rows = ATT_GROUP * blk
    bias = bias_ref[0]
    grp = lax.broadcasted_iota(jnp.int32, (rows, 1), 0) // blk
    tabs = [(t_ref[:, 0:HEAD_DIM], t_ref[:, HEAD_DIM:2 * HEAD_DIM]) for t_ref in (tp_ref, tc_ref, tn_ref)]
    cosc, sinc = tabs[1]
    scale = HEAD_DIM ** -0.5
    ones = jnp.ones((3 * blk, HEAD_DIM), BF16)
    for g in range(ATT_KV_HEADS):
        ks = slice(g * HEAD_DIM, (g + 1) * HEAD_DIM)
        vs = slice(ATT_KV + g * HEAD_DIM, ATT_KV + (g + 1) * HEAD_DIM)
        kband = jnp.concatenate(
            [_rope(kv_ref[:, ks].astype(F32), cos_t, sin_t).astype(BF16)
             for kv_ref, (cos_t, sin_t) in zip((kvp_ref, kvc_ref, kvn_ref), tabs)], axis=0)
        vband = jnp.concatenate(
            [jnp.concatenate([kvp_ref[:, vs], kvc_ref[:, vs], kvn_ref[:, vs]], axis=0), ones], axis=1)
        qs = []
        sink_col = jnp.zeros((rows, 1), F32)
        for j in range(ATT_GROUP):
            h = g * ATT_GROUP + j
            qh = q_ref[:, h * HEAD_DIM:(h + 1) * HEAD_DIM].astype(F32)
            qs.append((_rope(qh, cosc, sinc) * scale).astype(BF16))
            sink_col = jnp.where(grp == j, sink_ref[h], sink_col)
        qg = jnp.concatenate(qs, axis=0)
        s = _dot_nt(qg, kband) + bias
        m = jnp.maximum(jnp.max(s, axis=-1, keepdims=True), sink_col)
        p = jnp.exp(s - m)
        ov = _dot(p.astype(BF16), vband)
        den = ov[:, HEAD_DIM:HEAD_DIM + 1] + jnp.exp(sink_col - m)
        o = ov[:, 0:HEAD_DIM] / den
        for j in range(ATT_GROUP):
            h = g * ATT_GROUP + j
            o_ref[:, h * HEAD_DIM:(h + 1) * HEAD_DIM] = o[j * blk:(j + 1) * blk].astype(o_ref.dtype)


def _rope_table(npos):
    half = HEAD_DIM // 2
    inv = ROPE_THETA ** (-np.arange(half, dtype=np.float64) / half)
    ang = np.arange(npos, dtype=np.float64)[:, None] * inv[None, :]
    cos, sin = np.cos(ang), np.sin(ang)
    return jnp.asarray(np.concatenate([cos, cos, -sin, sin], axis=1).astype(np.float32))


def _attention(z, sink, rope_tab, blk_var, blk_pos):
    t = z.shape[0]
    rows = ATT_GROUP * WINDOW_BLOCK
    blk = WINDOW_BLOCK
    nblk = t // blk
    npos = rope_tab.shape[0] // blk
    assert Z_AV == Z_AK + ATT_KV, "k and v heads are read as one column block"
    kvcol = Z_AK // (2 * ATT_KV)
    prev = lambda n: jnp.maximum(n - 1, 0)
    nxt = lambda n: jnp.minimum(n + 1, nblk - 1)
    pprev = lambda p: jnp.maximum(p - 1, 0)
    pnxt = lambda p: jnp.minimum(p + 1, npos - 1)
    tab = lambda f: pl.BlockSpec((blk, 2 * HEAD_DIM), lambda n, v, p: (f(p[n]), 0))
    same = lambda p: p
    grid_spec = pltpu.PrefetchScalarGridSpec(
        num_scalar_prefetch=2,
        grid=(nblk,),
        in_specs=[
            pl.BlockSpec(memory_space=pltpu.SMEM),
            pl.BlockSpec((1, rows, 3 * blk), lambda n, v, p: (v[n], 0, 0)),
            pl.BlockSpec((blk, ATT_Q), lambda n, v, p: (n, Z_AQ // ATT_Q)),
            pl.BlockSpec((blk, 2 * ATT_KV), lambda n, v, p: (prev(n), kvcol)),
            pl.BlockSpec((blk, 2 * ATT_KV), lambda n, v, p: (n, kvcol)),
            pl.BlockSpec((blk, 2 * ATT_KV), lambda n, v, p: (nxt(n), kvcol)),
            tab(pprev), tab(same), tab(pnxt),
        ],
        out_specs=pl.BlockSpec((blk, ATT_Q), lambda n, v, p: (n, 0)),
    )
    return pl.pallas_call(
        _attn_kernel,
        grid_spec=grid_spec,
        out_shape=jax.ShapeDtypeStruct((t, ATT_Q), BF16),
        compiler_params=_params("arbitrary"),
        name="win_attn",
    )(blk_var, blk_pos, sink, _attn_bias(), z, z, z, z, rope_tab, rope_tab, rope_tab)


GLA_CHUNK = 128
GLA_SUB = 16
GLA_NSUB = GLA_CHUNK // GLA_SUB


def _logsig(z):
    return -(jnp.maximum(-z, 0.0) + jnp.log1p(jnp.exp(-jnp.abs(z))))


def _mask_sums(masks, g):
    g_hi = g.astype(BF16)
    g_lo = (g - g_hi.astype(F32)).astype(BF16)
    return _dot(masks, g_hi) + _dot(masks, g_lo)


def _gla_intra_kernel(q_ref, k_ref, v_ref, l_ref, w2f_ref, bf_ref, w2b_ref, bb_ref,
                      qtf_ref, kdf_ref, qtb_ref, kdb_ref, decf_ref, decb_ref, oi_ref):
    c, sub, nsub = GLA_CHUNK, GLA_SUB, GLA_NSUB
    one = lambda m: jnp.where(m, 1.0, 0.0).astype(BF16)
    ri = lax.broadcasted_iota(jnp.int32, (c, c), 0)
    ci = lax.broadcasted_iota(jnp.int32, (c, c), 1)
    blk0 = (ri // sub) * sub
    bi = lax.broadcasted_iota(jnp.int32, (PACK, c), 0) * sub
    cj = lax.broadcasted_iota(jnp.int32, (PACK, c), 1)
    masks_f = jnp.concatenate([one(ci <= ri), one((ci > blk0) & (ci <= ri)), one(cj <= bi)], axis=0)
    masks_b = jnp.concatenate(
        [one(ci >= ri), one((ci >= ri) & (ci < blk0 + sub - 1)), one(cj >= bi + sub - 1)], axis=0)

    glow = l_ref[...]
    gf = _logsig(_dot(glow[:, 0:GLA_GATE_RANK], w2f_ref[...]) + bf_ref[...]) * (1.0 / GLA_TAU)
    gb = _logsig(_dot(glow[:, GLA_GATE_RANK:2 * GLA_GATE_RANK], w2b_ref[...]) + bb_ref[...]) * (1.0 / GLA_TAU)
    sums_f = _mask_sums(masks_f, gf)
    sums_b = _mask_sums(masks_b, gb)
    q = q_ref[...].astype(F32) * (GLA_DK ** -0.5)
    k = k_ref[...].astype(F32)

    rr = lax.broadcasted_iota(jnp.int32, (sub, c), 0)
    cc = lax.broadcasted_iota(jnp.int32, (sub, c), 1)
    zero_blk = jnp.zeros((sub, GLA_DK), BF16)

    def direction(sums, reverse, qt_ref, kd_ref, dec_ref):
        b, win, ref = sums[0:c], sums[c:2 * c], sums[2 * c:2 * c + PACK]
        edge = b[0:1] if reverse else b[c - 1:c]
        qt_ref[...] = (q * jnp.exp(b)).astype(BF16)
        kd_ref[...] = (k * jnp.exp(edge - b)).astype(BF16)
        dec_ref[0] = jnp.exp(edge)
        qw = (q * jnp.exp(win)).astype(BF16)
        kw = k * jnp.exp(-win)
        att = [[] for _ in range(GLA_HEADS)]
        for i in range(nsub):
            dmat = jnp.exp(jnp.minimum(ref[i:i + 1] - ref, 0.0))
            live = range(i, nsub) if reverse else range(0, i + 1)
            for hd in range(GLA_HEADS):
                sl = slice(hd * GLA_DK, (hd + 1) * GLA_DK)
                kh = jnp.concatenate(
                    [(kw[j * sub:(j + 1) * sub, sl] * dmat[j:j + 1, sl]).astype(BF16) if j in live else zero_blk
                     for j in range(nsub)], axis=0)
                a = _dot_nt(qw[i * sub:(i + 1) * sub, sl], kh)
                keep = (cc >= rr + i * sub) if reverse else (cc <= rr + i * sub)
                att[hd].append(jnp.where(keep, a, 0.0))
        return [jnp.concatenate(rows, axis=0) for rows in att]

    att_f = direction(sums_f, False, qtf_ref, kdf_ref, decf_ref)
    att_b = direction(sums_b, True, qtb_ref, kdb_ref, decb_ref)
    for hd in range(GLA_HEADS):
        vsl = slice(hd * GLA_DV, (hd + 1) * GLA_DV)
        oi_ref[:, vsl] = _dot((att_f[hd] + att_b[hd]).astype(BF16), v_ref[:, vsl]).astype(oi_ref.dtype)


def _gla_intra(z, w2f, bf, w2b, bb):
    t = z.shape[0]
    c = GLA_CHUNK
    nch = t // c
    zspec = lambda w, off: pl.BlockSpec((c, w), lambda n: (n, off // w))
    wspec = pl.BlockSpec((GLA_GATE_RANK, GLA_K), lambda n: (0, 0))
    bspec = pl.BlockSpec((1, GLA_K), lambda n: (0, 0))
    tok = lambda w: pl.BlockSpec((c, w), lambda n: (n, 0))
    dec = pl.BlockSpec((1, 1, GLA_K), lambda n: (n, 0, 0))
    tk = jax.ShapeDtypeStruct((t, GLA_K), BF16)
    dk = jax.ShapeDtypeStruct((nch, 1, GLA_K), F32)
    return pl.pallas_call(
        _gla_intra_kernel,
        grid=(nch,),
        in_specs=[zspec(GLA_K, Z_GQ), zspec(GLA_K, Z_GK), zspec(GLA_V, Z_GV), zspec(LANES, Z_GLOW),
                  wspec, bspec, wspec, bspec],
        out_specs=[tok(GLA_K), tok(GLA_K), tok(GLA_K), tok(GLA_K), dec, dec, tok(GLA_V)],
        out_shape=[tk, tk, tk, tk, dk, dk, jax.ShapeDtypeStruct((t, GLA_V), BF16)],
        compiler_params=_params("arbitrary"),
        name="gla_intra",
    )(z, z, z, z, w2f, bf, w2b, bb)


def _gla_state_kernel(first_ref, last_ref, qf_ref, kf_ref, vf_ref, df_ref, qb_ref, kb_ref, vb_ref, db_ref, oi_ref,
                      of_ref, ob_ref, sf_ref, sb_ref):
    n = pl.program_id(0)
    nch = pl.num_programs(0)

    @pl.when(first_ref[n] == 1)
    def _():
        sf_ref[...] = jnp.zeros_like(sf_ref)

    @pl.when(last_ref[nch - 1 - n] == 1)
    def _():
        sb_ref[...] = jnp.zeros_like(sb_ref)

    for hd in range(GLA_HEADS):
        ksl = slice(hd * GLA_DK, (hd + 1) * GLA_DK)
        vsl = slice(hd * GLA_DV, (hd + 1) * GLA_DV)
        for q_ref, k_ref, v_ref, d_ref, o_ref, s_ref in (
                (qf_ref, kf_ref, vf_ref, df_ref, of_ref, sf_ref), (qb_ref, kb_ref, vb_ref, db_ref, ob_ref, sb_ref)):
            state = s_ref[hd]
            inter = _dot_nt(q_ref[:, ksl], state.astype(BF16))
            if o_ref is of_ref:
                inter = inter + oi_ref[:, vsl].astype(F32)
            o_ref[:, vsl] = inter.astype(o_ref.dtype)
            s_ref[hd] = d_ref[0][:, ksl] * state + _dot_tn(v_ref[:, vsl], k_ref[:, ksl])


def _gla_state(z, qtf, kdf, decf, qtb, kdb, decb, o_intra, ch_first, ch_last):
    t = z.shape[0]
    c = GLA_CHUNK
    nch = t // c
    fwd = lambda n: n
    bwd = lambda n: nch - 1 - n
    tok = lambda f: pl.BlockSpec((c, GLA_K), lambda n, a, b: (f(n), 0))
    val = lambda f: pl.BlockSpec((c, GLA_V), lambda n, a, b: (f(n), Z_GV // GLA_V))
    dec = lambda f: pl.BlockSpec((1, 1, GLA_K), lambda n, a, b: (f(n), 0, 0))
    out = lambda f: pl.BlockSpec((c, GLA_V), lambda n, a, b: (f(n), 0))
    grid_spec = pltpu.PrefetchScalarGridSpec(
        num_scalar_prefetch=2,
        grid=(nch,),
        in_specs=[tok(fwd), tok(fwd), val(fwd), dec(fwd), tok(bwd), tok(bwd), val(bwd), dec(bwd), out(fwd)],
        out_specs=[out(fwd), out(bwd)],
        scratch_shapes=[pltpu.VMEM((GLA_HEADS, GLA_DV, GLA_DK), F32), pltpu.VMEM((GLA_HEADS, GLA_DV, GLA_DK), F32)],
    )
    o = jax.ShapeDtypeStruct((t, GLA_V), BF16)
    return pl.pallas_call(
        _gla_state_kernel,
        grid_spec=grid_spec,
        out_shape=[o, o],
        compiler_params=_params("arbitrary"),
        name="gla_state",
    )(ch_first, ch_last, qtf, kdf, z, decf, qtb, kdb, z, decb, o_intra)


def _merge_kernel(seq_ref, oa_ref, of_ref, ob_ref, gr_ref, ga_ref, gb_ref, xp_ref, xs_ref, nw_ref,
                  wa_ref, wb_ref, wo_ref, g1_ref, l1g_ref, l1b_ref, sh2_ref, sc2_ref, x1_ref, h2_ref,
                  *, n_prompt_tiles):
    i = pl.program_id(0)
    og = of_ref[...].astype(F32) + ob_ref[...].astype(F32)
    parts = []
    for h in range(GLA_HEADS):
        th = og[:, h * GLA_DV:(h + 1) * GLA_DV]
        ms = jnp.mean(th * th, axis=-1, keepdims=True)
        parts.append(th * lax.rsqrt(ms + RMS_EPS))
    gr = gr_ref[...].astype(F32)
    ogn = jnp.concatenate(parts, axis=1) * nw_ref[...] * (gr * jax.nn.sigmoid(gr))
    a = _dot(oa_ref[...], wa_ref[...])
    b = _dot(ogn.astype(BF16), wb_ref[...])
    merged = jax.nn.sigmoid(ga_ref[...].astype(F32)) * a + jax.nn.sigmoid(gb_ref[...].astype(F32)) * b
    mix = g1_ref[0] * _dot(merged.astype(BF16), wo_ref[...])

    def finish(x_ref):
        x1 = _ln(DN_ALPHA * x_ref[...] + mix) * l1g_ref[...] + l1b_ref[...]
        x1_ref[...] = x1
        h2_ref[...] = (_ln(x1) * (1.0 + sc2_ref[0]) + sh2_ref[0]).astype(BF16)

    @pl.when(i < n_prompt_tiles)
    def _():
        finish(xp_ref)

    @pl.when(i >= n_prompt_tiles)
    def _():
        finish(xs_ref)


def _merge(o_attn, o_f, o_b, z, xp, xs, mod3, norm_w, wa, wb, wo, ln1_g, ln1_b, tile_seq, tm):
    d = xp.shape[1]
    t = xp.shape[0] + xs.shape[0]
    npt = xp.shape[0] // tm
    tok = lambda col: pl.BlockSpec((tm, d), lambda i, s: (i, col))
    const = lambda shape: pl.BlockSpec(shape, lambda i, s: (0,) * len(shape), pipeline_mode=pl.Buffered(1))
    modspec = lambda k: pl.BlockSpec((1, 1, d), lambda i, s: (s[i] * 6 + k, 0, 0))
    pspec, sspec = _two_group_specs(tm, d, npt)
    grid_spec = pltpu.PrefetchScalarGridSpec(
        num_scalar_prefetch=1,
        grid=(t // tm,),
        in_specs=[tok(0), tok(0), tok(0), tok(Z_GR // d), tok(Z_GA // d), tok(Z_GB // d), pspec, sspec,
                  const((1, d)), const((d, d)), const((d, d)), const((d, d)),
                  modspec(2), const((1, d)), const((1, d)), modspec(3), modspec(4)],
        out_specs=[tok(0), tok(0)],
    )
    return pl.pallas_call(
        functools.partial(_merge_kernel, n_prompt_tiles=npt),
        grid_spec=grid_spec,
        out_shape=[jax.ShapeDtypeStruct((t, d), F32), jax.ShapeDtypeStruct((t, d), BF16)],
        compiler_params=_params("arbitrary"),
        name="merge_ln1",
    )(tile_seq, o_attn, o_f, o_b, z, z, z, xp, xs, norm_w, wa, wb, wo, mod3, ln1_g, ln1_b, mod3, mod3)


SLAB = 8
RANK_NONE = 127.0


def _slabs(s):
    return [s[i * SLAB:(i + 1) * SLAB] for i in range(s.shape[0] // SLAB)]


def _extract_desc(slabs, count):
    vals = []
    for _ in range(count):
        m8 = functools.reduce(jnp.maximum, slabs)
        m = jnp.max(m8, axis=0, keepdims=True)
        vals.append(m)
        slabs = [jnp.where(sl == m, -jnp.inf, sl) for sl in slabs]
    return vals


def _sorting_network(n):
    pairs = []
    p = 1
    while p < n:
        k = p
        while k >= 1:
            for j in range(k % p, n - k, 2 * k):
                for i in range(min(k, n - j - k)):
                    if (i + j) // (2 * p) == (i + j + k) // (2 * p):
                        pairs.append((i + j, i + j + k))
            k //= 2
        p *= 2
    return pairs


def _extract_sorted(slabs, count):
    cols = list(slabs)
    for i, j in _sorting_network(len(cols)):
        cols[i], cols[j] = jnp.maximum(cols[i], cols[j]), jnp.minimum(cols[i], cols[j])
    vals = []
    for it in range(count):
        m = jnp.max(cols[0], axis=0, keepdims=True)
        vals.append(m)
        hit = cols[0] == m
        live = count - 1 - it
        cols = [jnp.where(hit, cols[k + 1], cols[k]) for k in range(live)]
    return vals


def _stack_rows(vals, rows, lanes):
    ridx = lax.broadcasted_iota(jnp.int32, (rows, lanes), 0)
    acc = jnp.full((rows, lanes), -jnp.inf, F32)
    for i, v in enumerate(vals):
        acc = jnp.where(ridx == i, v, acc)
    return acc


def _router_kernel(h2_ref, wq_ref, sk_ref, rk_ref, e2_ref, cnt_ref, c_ref):
    tm = h2_ref.shape[0]
    kx = PEER_TOPK
    q = _dot(h2_ref[...], wq_ref[...]).astype(BF16)
    ridx = lax.broadcasted_iota(jnp.int32, (SLAB, tm), 0)
    for h in range(PEER_HEADS):
        s1 = _dot_nt(sk_ref[h, 0], q[:, (2 * h) * PEER_NKEYS:(2 * h + 1) * PEER_NKEYS])
        s2 = _dot_nt(sk_ref[h, 1], q[:, (2 * h + 1) * PEER_NKEYS:(2 * h + 2) * PEER_NKEYS])
        v1 = _extract_sorted(_slabs(s1), kx)
        v2 = _extract_sorted(_slabs(s2), kx)
        rank2 = jnp.full_like(s2, RANK_NONE)
        for b in reversed(range(kx)):
            rank2 = jnp.where(s2 >= v2[b], float(b), rank2)
        st1 = _stack_rows(v1, 2 * SLAB, tm)
        st2 = _stack_rows(v2, 2 * SLAB, tm)
        cand = []
        for a in range(SLAB):
            nb = kx // (a + 1)
            for sl in range((nb + SLAB - 1) // SLAB):
                piece = st2[sl * SLAB:(sl + 1) * SLAB] + v1[a]
                if nb < (sl + 1) * SLAB:
                    piece = jnp.where(ridx < nb - sl * SLAB, piece, -jnp.inf)
                cand.append(piece)
        cand.append(st1[SLAB:2 * SLAB] + v2[0])
        top = _extract_desc(cand, kx)
        zsum = functools.reduce(lambda acc, v: acc + jnp.exp(v - top[0]), top, jnp.zeros_like(top[0]))
        tau = top[kx - 1]
        cnt = jnp.zeros_like(s1)
        for b in range(SLAB):
            cnt = jnp.where(s1 + v2[b] >= tau, float(b + 1), cnt)
        best = jnp.full_like(v1[0], float(SLAB))
        for b in range(SLAB, kx):
            best = jnp.where(v1[0] + v2[b] >= tau, float(b + 1), best)
        cnt = jnp.where((s1 == v1[0]) & (cnt == float(SLAB)), best, cnt)
        packed = (PEER_NKEYS // PACK, PACK, tm)
        rk_ref[h] = rank2.astype(BF16).reshape(packed)
        e2_ref[h] = jnp.exp(s2 - v2[0]).astype(BF16).reshape(packed)
        cnt_ref[h] = cnt
        c_ref[h] = jnp.exp(s1 - v1[0]) / zsum


def _router(h2, wq, subkeys, tm):
    t, d = h2.shape
    nq = wq.shape[1]
    shape = (PEER_HEADS, PEER_NKEYS, t)
    ospec = pl.BlockSpec((PEER_HEADS, PEER_NKEYS, tm), lambda i: (0, 0, i))
    pshape = (PEER_HEADS, PEER_NKEYS // PACK, PACK, t)
    pspec = pl.BlockSpec((PEER_HEADS, PEER_NKEYS // PACK, PACK, tm), lambda i: (0, 0, 0, i))
    return pl.pallas_call(
        _router_kernel,
        grid=(t // tm,),
        in_specs=[
            pl.BlockSpec((tm, d), lambda i: (i, 0)),
            pl.BlockSpec((d, nq), lambda i: (0, 0), pipeline_mode=pl.Buffered(1)),
            pl.BlockSpec(subkeys.shape, lambda i: (0, 0, 0, 0), pipeline_mode=pl.Buffered(1)),
        ],
        out_specs=[pspec, pspec, ospec, ospec],
        out_shape=[jax.ShapeDtypeStruct(pshape, BF16), jax.ShapeDtypeStruct(pshape, BF16),
                   jax.ShapeDtypeStruct(shape, F32), jax.ShapeDtypeStruct(shape, F32)],
        compiler_params=_params("arbitrary"),
        name="peer_router",
    )(h2, wq, subkeys)


_SQRT_HALF = math.sqrt(0.5)


def _gelu(x):
    return 0.5 * x * (1.0 + lax.erf(x * _SQRT_HALF))


def _cast_t_kernel(x_ref, o_ref):
    o_ref[...] = x_ref[...].T.astype(o_ref.dtype)


def _cast_transpose(x, tn):
    n, d = x.shape
    return pl.pallas_call(
        _cast_t_kernel,
        grid=(n // tn,),
        in_specs=[pl.BlockSpec((tn, d), lambda j: (j, 0))],
        out_specs=pl.BlockSpec((None, d, tn), lambda j: (j, 0, 0)),
        out_shape=jax.ShapeDtypeStruct((n // tn, d, tn), BF16),
        compiler_params=_params("arbitrary"),
        name="cast_transpose",
    )(x)


def _peer_first_kernel(u_ref, h2_ref, o_ref):
    o_ref[...] = _dot_nt(u_ref[...], h2_ref[...])


def _peer_first(h2, u_bf, tm, tn):
    d = h2.shape[1]
    return pl.pallas_call(
        _peer_first_kernel,
        grid=(1,),
        in_specs=[pl.BlockSpec((tn, d), lambda i: (0, 0)), pl.BlockSpec((tm, d), lambda i: (0, 0))],
        out_specs=pl.BlockSpec((tn, tm), lambda i: (0, 0)),
        out_shape=jax.ShapeDtypeStruct((tn, tm), F32),
        compiler_params=_params("arbitrary"),
        name="peer_first",
    )(u_bf, h2)


def _peer_kernel(seq_ref, h2n_ref, un_ref, vt_ref, rk_ref, e2_ref, cnt_ref, c_ref, at_init_ref, x1_ref, g2_ref,
                 l2g_ref, l2b_ref, op_ref, os_ref, at0_ref, at1_ref, wt_ref, acc_ref, *, n_prompt_tiles):
    i = pl.program_id(0)
    j = pl.program_id(1)
    tm = h2n_ref.shape[0]
    ni = un_ref.shape[0] // PEER_NKEYS
    groups = PEER_NKEYS // PACK

    @pl.when((i == 0) & (j == 0))
    def _():
        at0_ref[...] = at_init_ref[...]

    @pl.when(j == 0)
    def _():
        acc_ref[...] = jnp.zeros_like(acc_ref)

    def step(at_cur, at_next):
        at_next[...] = _dot_nt(un_ref[...], h2n_ref[...])
        for ii in range(ni):
            rows = slice(ii * PEER_NKEYS, (ii + 1) * PEER_NKEYS)
            gate = None
            for h in range(PEER_HEADS):
                cnt = jnp.broadcast_to(cnt_ref[h, ii:ii + 1, :], (PACK, tm)).astype(BF16)
                coef = jnp.broadcast_to(c_ref[h, ii:ii + 1, :], (PACK, tm)).astype(BF16)
                sel = jnp.where(rk_ref[h] < cnt[None], e2_ref[h] * coef[None], jnp.zeros((), BF16))
                gate = sel if gate is None else gate + sel
            act = _gelu(at_cur[rows, :]).astype(BF16)
            wt_ref[rows, :] = gate.reshape(PEER_NKEYS, tm) * act
        acc_ref[...] += _dot(vt_ref[...], wt_ref[...])

    @pl.when(j % 2 == 0)
    def _():
        step(at0_ref, at1_ref)

    @pl.when(j % 2 == 1)
    def _():
        step(at1_ref, at0_ref)

    @pl.when(j == pl.num_programs(1) - 1)
    def _():
        ff = acc_ref[...].T
        y = _ln(DN_ALPHA * x1_ref[...] + g2_ref[0] * ff) * l2g_ref[...] + l2b_ref[...]

        @pl.when(i < n_prompt_tiles)
        def _():
            op_ref[...] = y

        @pl.when(i >= n_prompt_tiles)
        def _():
            os_ref[...] = y


def _peer(h2, u_bf, vt_bf, rk, e2, cnt, coef, x1, mod3, ln2_g, ln2_b, tile_seq, tm, tn, t_prompt):
    t, d = x1.shape
    ne = u_bf.shape[0]
    ni = tn // PEER_NKEYS
    nti, ntj = t // tm, ne // tn
    assert ntj % 2 == 0, "the two A^T buffers alternate by the parity of the expert-tile index"
    npt = t_prompt // tm
    groups = PEER_NKEYS // PACK
    at_init = _peer_first(h2, u_bf, tm, tn)
    nxt_i = lambda i, j: jnp.minimum(jnp.where(j == ntj - 1, i + 1, i), nti - 1)
    nxt_j = lambda j: jnp.where(j == ntj - 1, 0, j + 1)
    once = pl.Buffered(1)
    const = lambda shape: pl.BlockSpec(shape, lambda i, j, s: (0,) * len(shape), pipeline_mode=once)
    grid_spec = pltpu.PrefetchScalarGridSpec(
        num_scalar_prefetch=1,
        grid=(nti, ntj),
        in_specs=[
            pl.BlockSpec((tm, d), lambda i, j, s: (nxt_i(i, j), 0), pipeline_mode=once),
            pl.BlockSpec((tn, d), lambda i, j, s: (nxt_j(j), 0)),
            pl.BlockSpec((None, d, tn), lambda i, j, s: (j, 0, 0)),
            pl.BlockSpec((PEER_HEADS, groups, PACK, tm), lambda i, j, s: (0, 0, 0, i), pipeline_mode=once),
            pl.BlockSpec((PEER_HEADS, groups, PACK, tm), lambda i, j, s: (0, 0, 0, i), pipeline_mode=once),
            pl.BlockSpec((PEER_HEADS, ni, tm), lambda i, j, s: (0, j, i)),
            pl.BlockSpec((PEER_HEADS, ni, tm), lambda i, j, s: (0, j, i)),
            const((tn, tm)),
            pl.BlockSpec((tm, d), lambda i, j, s: (i, 0), pipeline_mode=once),
            pl.BlockSpec((1, 1, d), lambda i, j, s: (s[i] * 6 + 5, 0, 0)),
            const((1, d)), const((1, d)),
        ],
        out_specs=[pl.BlockSpec((tm, d), lambda i, j, s: (jnp.minimum(i, npt - 1), 0)),
                   pl.BlockSpec((tm, d), lambda i, j, s: (jnp.maximum(i - npt, 0), 0))],
        scratch_shapes=[pltpu.VMEM((tn, tm), F32), pltpu.VMEM((tn, tm), F32), pltpu.VMEM((tn, tm), BF16),
                        pltpu.VMEM((d, tm), F32)],
    )
    return pl.pallas_call(
        functools.partial(_peer_kernel, n_prompt_tiles=npt),
        grid_spec=grid_spec,
        out_shape=[jax.ShapeDtypeStruct((t_prompt, d), F32), jax.ShapeDtypeStruct((t - t_prompt, d), F32)],
        compiler_params=_params("arbitrary", "arbitrary"),
        name="peer_dense",
    )(tile_seq, h2, u_bf, vt_bf, rk, e2, cnt, coef, at_init, x1, mod3, ln2_g, ln2_b)


def _seq_tables(seq_lens, unit):
    sid, first, last, pos = [], [], [], []
    for s, n in enumerate(seq_lens):
        k = n // unit
        sid += [s] * k
        first += [1] + [0] * (k - 1)
        last += [0] * (k - 1) + [1]
        pos += list(range(k))
    mk = lambda v: jnp.asarray(np.asarray(v, np.int32))
    return mk(sid), mk(first), mk(last), mk(pos)


def _pick(n, options):
    for o in options:
        if n % o == 0:
            return o
    raise ValueError(f"no tile size in {options} divides {n}")


def _reorder_kernel(w_ref, o_ref):
    src = dict(zip(("aq", "ak", "av", "gq", "gk", "gv", "glow_f", "glow_b", "gr", "ga", "gb"),
                   np.concatenate([[0], np.cumsum(IN_WIDTHS)[:-1]]).tolist()))
    w = w_ref[...]
    cut = lambda name, width: w[:, src[name]:src[name] + width]
    lane = lax.broadcasted_iota(jnp.int32, (w.shape[0], LANES), 1)
    glow = jnp.where(lane < 2 * GLA_GATE_RANK, cut("glow_f", LANES), 0.0)
    pieces = [cut("aq", ATT_Q), cut("gv", GLA_V), cut("gr", GLA_V + 2 * D_MODEL), cut("gq", GLA_K), cut("gk", GLA_K),
              cut("ak", ATT_KV), cut("av", ATT_KV), glow]
    o_ref[...] = jnp.concatenate(pieces, axis=1).astype(o_ref.dtype)


def _reorder_w_in(w_in, tr=128):
    d, n = w_in.shape
    return pl.pallas_call(
        _reorder_kernel,
        grid=(d // tr,),
        in_specs=[pl.BlockSpec((tr, n), lambda i: (i, 0))],
        out_specs=pl.BlockSpec((tr, Z_WIDTH), lambda i: (i, 0)),
        out_shape=jax.ShapeDtypeStruct((d, Z_WIDTH), BF16),
        compiler_params=_params("arbitrary"),
        name="reorder_w_in",
    )(w_in)


def _layer(xp, xs, c_pad, seq_lens, w_ada, b_ada, w_in, attn_sink, gla_w2_fwd, gla_b_fwd, gla_w2_bwd, gla_b_bwd,
           gla_norm_w, w_branch_attn, w_branch_gla, w_out, ln1_g, ln1_b, peer_w_query, peer_subkeys,
           peer_u, peer_v, ln2_g, ln2_b):
    d = xp.shape[1]
    gcd_len = functools.reduce(math.gcd, seq_lens)
    tm_in = _pick(gcd_len, (512, 256, 128))
    tm_merge = _pick(gcd_len, (256, 128))
    tm_router = _pick(gcd_len, (256, 128))
    tm_peer = _pick(gcd_len, (512, 256, 128))

    mod = _ada(c_pad, w_ada, b_ada)
    mod3 = mod.reshape(c_pad.shape[0] * 6, 1, d)

    seq_in = _seq_tables(seq_lens, tm_in)[0]
    z = _ln_win(xp, xs, mod3, _reorder_w_in(w_in), seq_in, tm_in, Z_WIDTH // LN_WIN_COL_TILES)

    _, blk_first, blk_last, blk_pos = _seq_tables(seq_lens, WINDOW_BLOCK)
    o_attn = _attention(z, attn_sink.astype(F32), _rope_table(max(seq_lens)), blk_first + 2 * blk_last, blk_pos)

    _, ch_first, ch_last, _ = _seq_tables(seq_lens, GLA_CHUNK)
    qtf, kdf, qtb, kdb, decf, decb, o_intra = _gla_intra(
        z, gla_w2_fwd.astype(BF16), gla_b_fwd.reshape(1, GLA_K), gla_w2_bwd.astype(BF16), gla_b_bwd.reshape(1, GLA_K))
    o_f, o_b = _gla_state(z, qtf, kdf, decf, qtb, kdb, decb, o_intra, ch_first, ch_last)

    seq_merge = _seq_tables(seq_lens, tm_merge)[0]
    row = lambda v: v.reshape(1, d)
    x1, h2 = _merge(o_attn, o_f, o_b, z, xp, xs, mod3, row(gla_norm_w), w_branch_attn.astype(BF16),
                    w_branch_gla.astype(BF16), w_out.astype(BF16), row(ln1_g), row(ln1_b), seq_merge, tm_merge)

    rk, e2, cnt, coef = _router(h2, peer_w_query.astype(BF16), peer_subkeys.astype(BF16), tm_router)

    seq_peer = _seq_tables(seq_lens, tm_peer)[0]
    tn_peer = 1024
    return _peer(h2, peer_u.astype(BF16), _cast_transpose(peer_v, tn_peer), rk, e2, cnt, coef, x1, mod3,
                 row(ln2_g), row(ln2_b), seq_peer, tm_peer, tn_peer, xp.shape[0])


def kernel(x_prompt, x_sample, c_prompt, c_sample, w_ada, b_ada, w_in, attn_sink, gla_w2_fwd, gla_b_fwd,
           gla_w2_bwd, gla_b_bwd, gla_norm_w, w_branch_attn, w_branch_gla, w_out, ln1_g, ln1_b, peer_w_query,
           peer_subkeys, peer_u, peer_v, ln2_g, ln2_b):
    assert w_ada.shape[0] == 1, "single-layer trunk"
    bp, sp, d = x_prompt.shape
    bs, ss, _ = x_sample.shape
    seq_lens = [sp] * bp + [ss] * bs
    nseq = bp + bs
    c_pad = jnp.zeros((-(-nseq // 8) * 8, d), F32).at[:nseq].set(jnp.concatenate([c_prompt, c_sample], axis=0))
    yp, ys = _layer(x_prompt.reshape(bp * sp, d), x_sample.reshape(bs * ss, d), c_pad, seq_lens, w_ada[0], b_ada[0],
                    w_in[0], attn_sink[0], gla_w2_fwd[0], gla_b_fwd[0], gla_w2_bwd[0], gla_b_bwd[0], gla_norm_w[0],
                    w_branch_attn[0], w_branch_gla[0], w_out[0], ln1_g[0], ln1_b[0], peer_w_query[0],
                    peer_subkeys[0], peer_u[0], peer_v[0], ln2_g[0], ln2_b[0])
    return (yp.reshape(bp, sp, d), ys.reshape(bs, ss, d))
```
